```python
import jax
import jax.numpy as jnp
from jax import lax
import numpy as np

D_MODEL = 1024
BATCH = 4
SEQ = 4096
DEPTH = 4

HEAD_DIM = 64
RWKV_WIDTH = D_MODEL // 4
RWKV_HEADS = RWKV_WIDTH // HEAD_DIM
RWKV_DECAY_LORA = 64
RWKV_ICLR_LORA = 64
RWKV_GATE_LORA = 128
RWKV_COLS = 3 * RWKV_WIDTH + RWKV_GATE_LORA + 2 * RWKV_DECAY_LORA + 2 * RWKV_ICLR_LORA
MLA_V = 64
MLA_WIDTH = D_MODEL // 2
MLA_HEADS = MLA_WIDTH // MLA_V
MLA_NOPE = 64
MLA_ROPE = 32
MLA_Q_RANK = 384
MLA_KV_RANK = 256
MLA_COLS = MLA_Q_RANK + MLA_KV_RANK + MLA_ROPE
MLA_BLOCK = 128
RET_WIDTH = D_MODEL - RWKV_WIDTH - MLA_WIDTH
RET_HEADS = RET_WIDTH // HEAD_DIM
RET_COLS = 4 * RET_WIDTH
RET_CHUNK = 128

MIX_WIDTH = RWKV_WIDTH + MLA_WIDTH + RET_WIDTH
IN_COLS = RWKV_COLS + MLA_COLS + RET_COLS
D_FF = 2816
N_MOD = 9
ROPE_BASE = 10000.0
NORM_EPS = 1e-6
RWKV_LN_EPS = 64e-5
RET_LN_EPS = 1e-5

kernel_name = 'hybrid_rwkv7_mla_retention_encoder'


def split_cols(t, sizes):
    idx = np.cumsum(sizes)[:-1].tolist()
    return jnp.split(t, idx, axis=-1)


def rms_norm(t, eps=NORM_EPS):
    tf = t.astype(jnp.float32)
    return (tf * lax.rsqrt(jnp.mean(tf * tf, axis=-1, keepdims=True) + eps)).astype(t.dtype)


def head_norm(t, eps):
    tf = t.astype(jnp.float32)
    mu = jnp.mean(tf, axis=-1, keepdims=True)
    var = jnp.mean(jnp.square(tf - mu), axis=-1, keepdims=True)
    return (tf - mu) * lax.rsqrt(var + eps)


def modulate(t, shift, scale):
    return t * (1.0 + scale[:, None, :]) + shift[:, None, :]


def swiglu(h, w_in, w_out):
    gate, up = jnp.split(h @ w_in, 2, axis=-1)
    return (jax.nn.silu(gate) * up) @ w_out


def rotary(t, positions):
    d = t.shape[-1]
    inv = ROPE_BASE ** (-jnp.arange(0, d, 2, dtype=jnp.float32) / d)
    ang = positions.astype(jnp.float32)[..., None] * inv
    ang = ang.reshape(ang.shape[:2] + (1,) * (t.ndim - 3) + ang.shape[-1:])
    cos, sin = jnp.cos(ang).astype(t.dtype), jnp.sin(ang).astype(t.dtype)
    t1, t2 = t[..., : d // 2], t[..., d // 2:]
    return jnp.concatenate([t1 * cos - t2 * sin, t2 * cos + t1 * sin], axis=-1)


def centred_shift(t, mu):
    prev = jnp.pad(t[:, :-1], ((0, 0), (1, 0), (0, 0)))
    nxt = jnp.pad(t[:, 1:], ((0, 0), (0, 1), (0, 0)))
    return t + mu * (0.5 * (prev + nxt) - t)


def _dir_stack(t):
    t = jnp.stack([t[:, :, 0], jnp.flip(t[:, :, 1], axis=1)], axis=2)
    return jnp.transpose(t, (1, 2, 0, 3, 4))


def rwkv7_mixer(z, mu, w0, w_up, a0, a_up, g_up, k_k, k_a, r_k, ln_g, ln_b):
    B, S, _ = z.shape
    H, N = RWKV_HEADS, HEAD_DIM
    z = centred_shift(z, mu)
    r, k, v, g_lo, w_lo, a_lo = split_cols(
        z, [RWKV_WIDTH, RWKV_WIDTH, RWKV_WIDTH, RWKV_GATE_LORA, 2 * RWKV_DECAY_LORA, 2 * RWKV_ICLR_LORA])
    w_lo = w_lo.reshape(B, S, 2, RWKV_DECAY_LORA)
    a_lo = a_lo.reshape(B, S, 2, RWKV_ICLR_LORA)
    w_raw = (w0 + jnp.einsum('bsdr,drc->bsdc', jnp.tanh(w_lo), w_up)).astype(jnp.float32)
    decay = jnp.exp(-jnp.exp(-jax.nn.softplus(-w_raw) - 0.5)).reshape(B, S, 2, H, N)
    a = jax.nn.sigmoid(a0 + jnp.einsum('bsdr,drc->bsdc', a_lo, a_up))
    g = jax.nn.sigmoid(g_lo) @ g_up
    kk = (k * k_k).reshape(B, S, H, N).astype(jnp.float32)
    kk = kk / jnp.maximum(jnp.linalg.norm(kk, axis=-1, keepdims=True), 1e-12)
    kk2 = jnp.broadcast_to(kk[:, :, None], (B, S, 2, H, N))
    a5 = a.reshape(B, S, 2, H, N)
    k_mod = (k[:, :, None, :] * (1.0 + (a - 1.0) * k_a)).reshape(B, S, 2, H, N)
    rh, kh, vh = (t.reshape(B, S, H, N) for t in (r, k, v))
    r2 = jnp.broadcast_to(rh[:, :, None], (B, S, 2, H, N))
    v2 = jnp.broadcast_to(vh[:, :, None], (B, S, 2, H, N))
    xs = tuple(_dir_stack(t) for t in (decay, kk2, kk2 * a5, k_mod, v2, r2))

    def step(state, inp):
        w_t, kk_t, kka_t, k_t, v_t, r_t = inp
        sk = jnp.einsum('dbhvk,dbhk->dbhv', state, kk_t)
        new = (state * w_t[..., None, :] - sk[..., :, None] * kka_t[..., None, :]
               + v_t[..., :, None] * k_t[..., None, :])
        y_fwd = jnp.einsum('bhvk,bhk->bhv', new[0], r_t[0])
        y_bwd = jnp.einsum('bhvk,bhk->bhv', state[1], r_t[1])
        return new, jnp.stack([y_fwd, y_bwd])

    state0 = jnp.zeros((2, B, H, N, N), jnp.float32)
    _, ys = lax.scan(step, state0, xs)
    y = ys[:, 0] + jnp.flip(ys[:, 1], axis=0)
    y = jnp.transpose(y, (1, 0, 2, 3))
    y = (head_norm(y, RWKV_LN_EPS) * ln_g + ln_b).astype(z.dtype)
    bonus = jnp.sum(rh * kh * r_k, axis=-1, keepdims=True) * vh
    return (y + bonus).reshape(B, S, RWKV_WIDTH) * g


def mla_mixer(z, positions, q_norm_g, w_uq, kv_norm_g, w_ukv):
    B, S, _ = z.shape
    H = MLA_HEADS
    cq, ckv, kpe = split_cols(z, [MLA_Q_RANK, MLA_KV_RANK, MLA_ROPE])
    q = ((rms_norm(cq) * q_norm_g) @ w_uq).reshape(B, S, H, MLA_NOPE + MLA_ROPE)
    kv = ((rms_norm(ckv) * kv_norm_g) @ w_ukv).reshape(B, S, H, MLA_NOPE + MLA_V)
    q_nope, q_pe = q[..., :MLA_NOPE], rotary(q[..., MLA_NOPE:], positions)
    k_nope, v = kv[..., :MLA_NOPE], kv[..., MLA_NOPE:]
    k_pe = rotary(kpe, positions)
    scale = (MLA_NOPE + MLA_ROPE) ** -0.5
    nb = S // MLA_BLOCK
    qn_b = jnp.transpose(q_nope.reshape(B, nb, MLA_BLOCK, H, MLA_NOPE), (1, 0, 2, 3, 4))
    qp_b = jnp.transpose(q_pe.reshape(B, nb, MLA_BLOCK, H, MLA_ROPE), (1, 0, 2, 3, 4))

    def attend(blk):
        qn, qp = blk
        s = (jnp.einsum('bqhd,bkhd->bhqk', qn, k_nope)
             + jnp.einsum('bqhr,bkr->bhqk', qp, k_pe))
        p = jax.nn.softmax(s.astype(jnp.float32) * scale, axis=-1).astype(v.dtype)
        return jnp.einsum('bhqk,bkhd->bqhd', p, v)

    o = lax.map(attend, (qn_b, qp_b))
    return jnp.transpose(o, (1, 0, 2, 3, 4)).reshape(B, S, MLA_WIDTH)


def retention_dir(q, k, v, log_gamma, strict):
    B, H, S, d = q.shape
    C = RET_CHUNK
    n = S // C
    qc, kc, vc = (t.reshape(B, H, n, C, d) for t in (q, k, v))
    lg = log_gamma.astype(jnp.float32)
    pos = jnp.arange(C, dtype=jnp.float32)
    diff = pos[:, None] - pos[None, :]
    mask = diff > 0 if strict else diff >= 0
    dmat = jnp.where(mask, jnp.exp(lg[:, None, None] * jnp.maximum(diff, 0.0)), 0.0)
    scores = jnp.einsum('bhncd,bhnmd->bhncm', qc, kc) * dmat[None, :, None]
    inner = jnp.einsum('bhncm,bhnme->bhnce', scores, vc)
    k_w = kc * jnp.exp(lg[:, None] * (C - 1.0 - pos)[None, :])[None, :, None, :, None]
    kv = jnp.einsum('bhncd,bhnce->nbhde', k_w, vc)
    chunk_decay = jnp.exp(lg * C)[None, :, None, None]

    def step(R, kv_n):
        return R * chunk_decay + kv_n, R

    _, r_prev = lax.scan(step, jnp.zeros((B, H, d, d), jnp.float32), kv)
    q_w = qc * jnp.exp(lg[:, None] * (pos + 1.0)[None, :])[None, :, None, :, None]
    cross = jnp.einsum('bhncd,nbhde->bhnce', q_w, r_prev)
    return (inner + cross).reshape(B, H, S, d)


def retention_mixer(z, positions, log_rate, gn_g):
    B, S, _ = z.shape
    H, d = RET_HEADS, HEAD_DIM
    q, k, v, gate = split_cols(z, [RET_WIDTH] * 4)
    q = rotary(q.reshape(B, S, H, d), positions)
    k = rotary(k.reshape(B, S, H, d), positions) * (d ** -0.5)
    v = v.reshape(B, S, H, d)
    q, k, v = (jnp.transpose(t, (0, 2, 1, 3)) for t in (q, k, v))
    log_gamma = -jnp.exp(log_rate.astype(jnp.float32))
    y_f = retention_dir(q, k, v, log_gamma[0], False)
    y_b = jnp.flip(retention_dir(jnp.flip(q, 2), jnp.flip(k, 2), jnp.flip(v, 2), log_gamma[1], True), 2)
    y = jnp.transpose(y_f + y_b, (0, 2, 1, 3))
    y = (head_norm(y, RET_LN_EPS) * gn_g).astype(z.dtype).reshape(B, S, RET_WIDTH)
    return jax.nn.silu(gate) * y


def setup_inputs(seed: int = 0) -> dict:
    key = jax.random.key(seed)
    ks = jax.random.split(key, 32)
    L, D = DEPTH, D_MODEL
    C = RWKV_WIDTH

    def nrm(k, shape, scale):
        return jax.random.normal(k, shape, jnp.float32) * scale

    ret_base = jnp.log(2.0 ** (-5.0 - jnp.arange(RET_HEADS, dtype=jnp.float32)))
    return {
        'x': nrm(ks[0], (BATCH, SEQ, D), 1.0),
        'c': nrm(ks[1], (BATCH, D), 1.0),
        'positions': jnp.tile(jnp.arange(SEQ, dtype=jnp.int32)[None, :], (BATCH, 1)),
        'w_ada': nrm(ks[2], (L, D, N_MOD * D), 0.5 * D ** -0.5),
        'b_ada': nrm(ks[3], (L, N_MOD * D), 0.02),
        'w_ff1_in': nrm(ks[4], (L, D, 2 * D_FF), D ** -0.5),
        'w_ff1_out': nrm(ks[5], (L, D_FF, D), D_FF ** -0.5),
        'w_ff2_in': nrm(ks[6], (L, D, 2 * D_FF), D ** -0.5),
        'w_ff2_out': nrm(ks[7], (L, D_FF, D), D_FF ** -0.5),
        'w_in': nrm(ks[8], (L, D, IN_COLS), D ** -0.5),
        'w_out': nrm(ks[9], (L, MIX_WIDTH, D), MIX_WIDTH ** -0.5),
        'rwkv_mu': jax.random.uniform(ks[10], (L, RWKV_COLS), jnp.float32),
        'rwkv_w0': jax.random.uniform(ks[11], (L, 2, C), jnp.float32, minval=-5.0, maxval=1.0),
        'rwkv_w_up': nrm(ks[12], (L, 2, RWKV_DECAY_LORA, C), 0.5 * RWKV_DECAY_LORA ** -0.5),
        'rwkv_a0': nrm(ks[13], (L, 2, C), 0.5),
        'rwkv_a_up': nrm(ks[14], (L, 2, RWKV_ICLR_LORA, C), 0.5 * RWKV_ICLR_LORA ** -0.5),
        'rwkv_g_up': nrm(ks[15], (L, RWKV_GATE_LORA, C), RWKV_GATE_LORA ** -0.5),
        'rwkv_k_k': 0.85 + nrm(ks[16], (L, C), 0.05),
        'rwkv_k_a': 1.0 + nrm(ks[17], (L, C), 0.05),
        'rwkv_r_k': nrm(ks[18], (L, RWKV_HEADS, HEAD_DIM), 0.1),
        'rwkv_ln_g': 1.0 + nrm(ks[19], (L, RWKV_HEADS, HEAD_DIM), 0.05),
        'rwkv_ln_b': nrm(ks[20], (L, RWKV_HEADS, HEAD_DIM), 0.02),
        'mla_q_norm_g': 1.0 + nrm(ks[21], (L, MLA_Q_RANK), 0.05),
        'mla_w_uq': nrm(ks[22], (L, MLA_Q_RANK, MLA_HEADS * (MLA_NOPE + MLA_ROPE)), MLA_Q_RANK ** -0.5),
        'mla_kv_norm_g': 1.0 + nrm(ks[23], (L, MLA_KV_RANK), 0.05),
        'mla_w_ukv': nrm(ks[24], (L, MLA_KV_RANK, MLA_HEADS * (MLA_NOPE + MLA_V)), MLA_KV_RANK ** -0.5),
        'ret_log_rate': ret_base + nrm(ks[25], (L, 2, RET_HEADS), 0.1),
        'ret_gn_g': 1.0 + nrm(ks[26], (L, RET_HEADS, HEAD_DIM), 0.05),
        'final_norm_g': 1.0 + nrm(ks[27], (D,), 0.05),
    }


def reference(x, c, positions, w_ada, b_ada, w_ff1_in, w_ff1_out, w_ff2_in, w_ff2_out,
              w_in, w_out, rwkv_mu, rwkv_w0, rwkv_w_up, rwkv_a0, rwkv_a_up, rwkv_g_up,
              rwkv_k_k, rwkv_k_a, rwkv_r_k, rwkv_ln_g, rwkv_ln_b, mla_q_norm_g, mla_w_uq,
              mla_kv_norm_g, mla_w_ukv, ret_log_rate, ret_gn_g, final_norm_g):
    cond = jax.nn.silu(c)
    for l in range(DEPTH):
        mod = cond @ w_ada[l] + b_ada[l]
        sh1, sc1, g1, sh2, sc2, g2, sh3, sc3, g3 = jnp.split(mod, N_MOD, axis=-1)
        h = modulate(rms_norm(x), sh1, sc1)
        x = x + 0.5 * g1[:, None, :] * swiglu(h, w_ff1_in[l], w_ff1_out[l])
        h = modulate(rms_norm(x), sh2, sc2)
        z = h @ w_in[l]
        z_a, z_b, z_c = split_cols(z, [RWKV_COLS, MLA_COLS, RET_COLS])
        o_a = rwkv7_mixer(z_a, rwkv_mu[l], rwkv_w0[l], rwkv_w_up[l], rwkv_a0[l], rwkv_a_up[l],
                          rwkv_g_up[l], rwkv_k_k[l], rwkv_k_a[l], rwkv_r_k[l], rwkv_ln_g[l], rwkv_ln_b[l])
        o_b = mla_mixer(z_b, positions, mla_q_norm_g[l], mla_w_uq[l], mla_kv_norm_g[l], mla_w_ukv[l])
        o_c = retention_mixer(z_c, positions, ret_log_rate[l], ret_gn_g[l])
        mixed = jnp.concatenate([o_a, o_b, o_c], axis=-1) @ w_out[l]
        x = x + g2[:, None, :] * mixed
        h = modulate(rms_norm(x), sh3, sc3)
        x = x + 0.5 * g3[:, None, :] * swiglu(h, w_ff2_in[l], w_ff2_out[l])
    return rms_norm(x) * final_norm_g
```

```python
import functools
import math

import jax
import jax.numpy as jnp
from jax import lax
from jax.experimental import pallas as pl
from jax.experimental.pallas import tpu as pltpu

F32 = jnp.float32
BF16 = jnp.bfloat16
HIGHEST = lax.Precision.HIGHEST

D_MODEL = 1024
HEAD_DIM = 64
RWKV_WIDTH = 256
RWKV_HEADS = 4
RWKV_LORA = 64
RWKV_GATE_LORA = 128
RWKV_COLS = 1152
MLA_HEADS = 8
MLA_NOPE = 64
MLA_ROPE = 32
MLA_V = 64
MLA_Q_RANK = 384
MLA_KV_RANK = 256
MLA_WIDTH = 512
MLA_HEAD_PAD = 128
RET_WIDTH = 256
RET_HEADS = 4
RET_CHUNK = 128
RWKV_CHUNK = 64
D_FF = 2816
N_MOD = 9
ROPE_BASE = 10000.0
NORM_EPS = 1e-6
RWKV_LN_EPS = 64e-5
RET_LN_EPS = 1e-5

VMEM_LIMIT = 48 * 1024 * 1024


def _cparams(*sem):
    return pltpu.CompilerParams(dimension_semantics=sem, vmem_limit_bytes=VMEM_LIMIT)


def _sigmoid(x):
    return 1.0 / (1.0 + jnp.exp(-x))


def _dot(a, b):
    return jnp.dot(a, b, preferred_element_type=F32)


def _dot_hi(a, b):
    return jnp.dot(a, b, precision=HIGHEST, preferred_element_type=F32)


def _split(a):
    hi = a.astype(BF16)
    lo = (a - hi.astype(F32)).astype(BF16)
    return hi, lo


def _dg3(a, b, dims):
    ah, al = _split(a)
    bh, bl = _split(b)
    dn = (dims, ((), ()))
    out = lax.dot_general(ah, bh, dn, preferred_element_type=F32)
    out = out + lax.dot_general(ah, bl, dn, preferred_element_type=F32)
    out = out + lax.dot_general(al, bh, dn, preferred_element_type=F32)
    return out


_NN = ((1,), (0,))
_NT = ((1,), (1,))
_TN = ((0,), (0,))


def _rms_mod(x, sh, sc):
    ms = jnp.mean(x * x, axis=-1, keepdims=True)
    return (x * lax.rsqrt(ms + NORM_EPS)) * (1.0 + sc) + sh


def _mod_kernel(c_ref, w_ref, b_ref, o_ref):
    c = c_ref[...]
    cond = c * _sigmoid(c)
    o_ref[0] = _dot_hi(cond, w_ref[0]) + b_ref[0]


def _mod_call(c, w_ada, b_ada):
    L, D, N = w_ada.shape
    B = c.shape[0]
    tn = 1152
    return pl.pallas_call(
        _mod_kernel,
        grid=(L, N // tn),
        in_specs=[
            pl.BlockSpec((B, D), lambda l, j: (0, 0)),
            pl.BlockSpec((1, D, tn), lambda l, j: (l, 0, j)),
            pl.BlockSpec((1, 1, tn), lambda l, j: (l, 0, j)),
        ],
        out_specs=pl.BlockSpec((1, B, tn), lambda l, j: (l, 0, j)),
        out_shape=jax.ShapeDtypeStruct((L, B, N), F32),
        compiler_params=_cparams("parallel", "parallel"),
        name="adaln_mod",
    )(c, w_ada, b_ada.reshape(L, 1, N))


def _rope_kernel(pos_ref, inv_ref, mc_ref, m1_ref, ma_ref, mb_ref, o_ref):
    pos = pos_ref[0].astype(F32)
    for i in range(2):
        ang = pos * inv_ref[i:i + 1, :]
        cos = jnp.cos(ang)
        sin = jnp.sin(ang)
        o_ref[i, 0, :, 0:128] = cos * mc_ref[i:i + 1, :] + m1_ref[i:i + 1, :]
        o_ref[i, 0, :, 128:256] = sin * ma_ref[i:i + 1, :]
        o_ref[i, 0, :, 256:384] = sin * mb_ref[i:i + 1, :]


def _rope_consts():
    lane = jnp.arange(128)
    inv_m = ROPE_BASE ** (-jnp.arange(0, MLA_ROPE, 2, dtype=F32) / MLA_ROPE)
    in_rope = (lane >= 64) & (lane < 96)
    inv0 = jnp.where(in_rope, inv_m[(lane - 64) % 16], 0.0)
    mc0 = in_rope.astype(F32)
    m10 = (lane < 64).astype(F32)
    ma0 = jnp.where((lane >= 64) & (lane < 80), -1.0, 0.0)
    mb0 = jnp.where((lane >= 80) & (lane < 96), 1.0, 0.0)
    inv_r = ROPE_BASE ** (-jnp.arange(0, HEAD_DIM, 2, dtype=F32) / HEAD_DIM)
    inv1 = inv_r[lane % 32]
    mc1 = jnp.ones((128,), F32)
    m11 = jnp.zeros((128,), F32)
    ma1 = jnp.where((lane % 64) < 32, -1.0, 0.0)
    mb1 = jnp.where((lane % 64) >= 32, 1.0, 0.0)
    st = lambda a, b: jnp.stack([a, b]).astype(F32)
    return st(inv0, inv1), st(mc0, mc1), st(m10, m11), st(ma0, ma1), st(mb0, mb1)


def _rope_call(positions):
    B, S = positions.shape
    ts = min(S, 512)
    consts = _rope_consts()
    cspec = pl.BlockSpec((2, 128), lambda b, i: (0, 0))
    return pl.pallas_call(
        _rope_kernel,
        grid=(B, S // ts),
        in_specs=[pl.BlockSpec((1, ts, 1), lambda b, i: (b, i, 0))] + [cspec] * 5,
        out_specs=pl.BlockSpec((2, 1, ts, 384), lambda b, i: (0, b, i, 0)),
        out_shape=jax.ShapeDtypeStruct((2, B, S, 384), F32),
        compiler_params=_cparams("parallel", "parallel"),
        name="rope_tables",
    )(positions.reshape(B, S, 1), *consts)


def _apply_rope(x, tab, shift):
    n = x.shape[1] // 128
    width = x.shape[1]
    cc = jnp.tile(tab[:, 0:128], (1, n))
    sa = jnp.tile(tab[:, 128:256], (1, n))
    sb = jnp.tile(tab[:, 256:384], (1, n))
    return x * cc + pltpu.roll(x, width - shift, axis=1) * sa + pltpu.roll(x, shift, axis=1) * sb


def _ffn_kernel(x_ref, sh_ref, sc_ref, g_ref, wg_ref, wu_ref, wo_ref, fg_ref, o_ref, h_scr, acc_scr, *, nj, final):
    j = pl.program_id(2)

    @pl.when(j == 0)
    def _():
        h = _rms_mod(x_ref[0], sh_ref[0], sc_ref[0])
        h_scr[...] = h.astype(BF16)
        acc_scr[...] = jnp.zeros_like(acc_scr)

    h = h_scr[...]
    gate = _dot(h, wg_ref[...])
    up = _dot(h, wu_ref[...])
    act = (gate * _sigmoid(gate) * up).astype(BF16)
    acc_scr[...] += _dot(act, wo_ref[...])

    @pl.when(j == nj - 1)
    def _():
        y = x_ref[0] + (0.5 * g_ref[0]) * acc_scr[...]
        if final:
            ms = jnp.mean(y * y, axis=-1, keepdims=True)
            y = (y * lax.rsqrt(ms + NORM_EPS)) * fg_ref[...]
        o_ref[0] = y


def _ffn_call(x, sh, sc, g, w_in_bf, w_out_bf, final_g, final):
    B, S, D = x.shape
    tm = min(S, 512)
    tf = 1408
    nj = D_FF // tf
    vec = pl.BlockSpec((1, 1, D), lambda b, i, j: (b, 0, 0))
    return pl.pallas_call(
        functools.partial(_ffn_kernel, nj=nj, final=final),
        grid=(B, S // tm, nj),
        in_specs=[
            pl.BlockSpec((1, tm, D), lambda b, i, j: (b, i, 0)),
            vec, vec, vec,
            pl.BlockSpec((D, tf), lambda b, i, j: (0, j)),
            pl.BlockSpec((D, tf), lambda b, i, j: (0, j + nj)),
            pl.BlockSpec((tf, D), lambda b, i, j: (j, 0)),
            pl.BlockSpec((1, D), lambda b, i, j: (0, 0)),
        ],
        out_specs=pl.BlockSpec((1, tm, D), lambda b, i, j: (b, i, 0)),
        out_shape=jax.ShapeDtypeStruct((B, S, D), F32),
        scratch_shapes=[pltpu.VMEM((tm, D), BF16), pltpu.VMEM((tm, D), F32)],
        compiler_params=_cparams("parallel", "parallel", "arbitrary"),
        name="ffn",
    )(x, sh, sc, g, w_in_bf, w_in_bf, w_out_bf, final_g)


ZB_COLS = MLA_Q_RANK + MLA_KV_RANK + MLA_HEAD_PAD
ZC_COLS = 4 * RET_WIDTH
Z_COLS = RWKV_COLS + ZB_COLS + ZC_COLS


def _inproj_kernel(x_ref, sh_ref, sc_ref, w_ref, za_ref, zb_ref, zc_ref):
    h = _rms_mod(x_ref[0], sh_ref[0], sc_ref[0]).astype(BF16)
    z = _dot(h, w_ref[...])
    za_ref[0] = z[:, 0:RWKV_COLS]
    zb_ref[0] = z[:, RWKV_COLS:RWKV_COLS + ZB_COLS]
    zc_ref[0] = z[:, RWKV_COLS + ZB_COLS:Z_COLS]


def _inproj_call(x, sh, sc, wz_bf):
    B, S, D = x.shape
    tm = min(S, 512)
    vec = pl.BlockSpec((1, 1, D), lambda b, i: (b, 0, 0))
    out = lambda n: pl.BlockSpec((1, tm, n), lambda b, i: (b, i, 0))
    return pl.pallas_call(
        _inproj_kernel,
        grid=(B, S // tm),
        in_specs=[
            pl.BlockSpec((1, tm, D), lambda b, i: (b, i, 0)),
            vec, vec,
            pl.BlockSpec((D, Z_COLS), lambda b, i: (0, 0)),
        ],
        out_specs=[out(RWKV_COLS), out(ZB_COLS), out(ZC_COLS)],
        out_shape=[jax.ShapeDtypeStruct((B, S, n), F32) for n in (RWKV_COLS, ZB_COLS, ZC_COLS)],
        compiler_params=_cparams("parallel", "parallel"),
        name="inproj",
    )(x, sh, sc, wz_bf)


def _rwkv_prep_kernel(z_ref, zp_ref, zn_ref, mu_ref, w0_ref, wup_ref, a0_ref, aup_ref, gup_ref,
                      kk_ref, ka_ref, rk_ref, ones_ref, pc_ref, pd_ref, pg_ref, *, ts, nt):
    i = pl.program_id(1)
    z = z_ref[0]
    row = lax.broadcasted_iota(jnp.int32, (ts, 1), 0)
    prev_edge = jnp.where(i > 0, zp_ref[0, 7:8, :], 0.0)
    next_edge = jnp.where(i < nt - 1, zn_ref[0, 0:1, :], 0.0)
    prev = jnp.where(row == 0, prev_edge, pltpu.roll(z, 1, axis=0))
    nxt = jnp.where(row == ts - 1, next_edge, pltpu.roll(z, ts - 1, axis=0))
    zs = z + mu_ref[...] * (0.5 * (prev + nxt) - z)

    r = zs[:, 0:256]
    k = zs[:, 256:512]
    v = zs[:, 512:768]
    g_lo = zs[:, 768:896]
    w_lo = zs[:, 896:1024]
    a_lo = zs[:, 1024:1152]

    w_raw = w0_ref[...] + _dg3(jnp.tanh(w_lo), wup_ref[...], _NN)
    logw = (-math.exp(-0.5)) * _sigmoid(w_raw)
    a = _sigmoid(a0_ref[...] + _dg3(a_lo, aup_ref[...], _NN))
    g = _dg3(_sigmoid(g_lo), gup_ref[...], _NN)

    ones_bd = ones_ref[...]
    kk0 = k * kk_ref[...]
    nrm = jnp.sqrt(_dot_hi(kk0 * kk0, ones_bd))
    kk = kk0 / jnp.maximum(nrm, 1e-12)
    bonus = _dot_hi(r * k * rk_ref[...], ones_bd) * v

    pc_ref[0, :, 0:256] = r
    pc_ref[0, :, 256:512] = v
    pc_ref[0, :, 512:768] = kk
    for d in range(2):
        a_d = a[:, 256 * d:256 * d + 256]
        pd_ref[d, 0, :, 0:256] = k * (1.0 + (a_d - 1.0) * ka_ref[...])
        pd_ref[d, 0, :, 256:512] = kk * a_d
        pd_ref[d, 0, :, 512:768] = logw[:, 256 * d:256 * d + 256]
    pg_ref[0, :, 0:256] = g
    pg_ref[0, :, 256:512] = bonus


def _rwkv_prep_call(za, mu, w0, wup_bd, a0, aup_bd, gup, k_k, k_a, r_k, ones_bd):
    B, S, _ = za.shape
    ts = min(S, 512)
    nt = S // ts
    nb8 = S // 8
    const = lambda shape: pl.BlockSpec(shape, lambda b, i: (0,) * len(shape))
    return pl.pallas_call(
        functools.partial(_rwkv_prep_kernel, ts=ts, nt=nt),
        grid=(B, nt),
        in_specs=[
            pl.BlockSpec((1, ts, RWKV_COLS), lambda b, i: (b, i, 0)),
            pl.BlockSpec((1, 8, RWKV_COLS), lambda b, i: (b, jnp.maximum(i * (ts // 8) - 1, 0), 0)),
            pl.BlockSpec((1, 8, RWKV_COLS), lambda b, i: (b, jnp.minimum((i + 1) * (ts // 8), nb8 - 1), 0)),
            const((1, RWKV_COLS)),
            const((1, 512)), const((128, 512)), const((1, 512)), const((128, 512)), const((128, 256)),
            const((1, 256)), const((1, 256)), const((1, 256)), const((256, 256)),
        ],
        out_specs=[
            pl.BlockSpec((1, ts, 768), lambda b, i: (b, i, 0)),
            pl.BlockSpec((2, 1, ts, 768), lambda b, i: (0, b, i, 0)),
            pl.BlockSpec((1, ts, 512), lambda b, i: (b, i, 0)),
        ],
        out_shape=[
            jax.ShapeDtypeStruct((B, S, 768), F32),
            jax.ShapeDtypeStruct((2, B, S, 768), F32),
            jax.ShapeDtypeStruct((B, S, 512), F32),
        ],
        compiler_params=_cparams("parallel", "parallel"),
        name="rwkv_prep",
    )(za, za, za, mu, w0, wup_bd, a0, aup_bd, gup, k_k, k_a, r_k, ones_bd)


def _rwkv_chunk(direction, pc, pd, s_ref, y_ref):
    C = RWKV_CHUNK
    r = pc[:, 0:256]
    v = pc[:, 256:512]
    kk = pc[:, 512:768]
    kmod = pd[:, 0:256]
    kka = pd[:, 256:512]
    logw = pd[:, 512:768]

    row = lax.broadcasted_iota(jnp.int32, (C, C), 0)
    col = lax.broadcasted_iota(jnp.int32, (C, C), 1)
    if direction == 0:
        earlier = col < row
        tri = (col <= row).astype(F32)
    else:
        earlier = col > row
        tri = (col >= row).astype(F32)
    eye = col == row

    l_in = _dot_hi(tri, logw)
    l_ex = l_in - logw
    l_tot = l_in[C - 1:C, :] if direction == 0 else l_in[0:1, :]
    e_ex = jnp.exp(l_ex)
    e_neg = jnp.exp(-l_in)
    e_tot = jnp.exp(l_tot - l_in)
    kkg = kk * e_ex
    bbar = kka * e_neg
    kbar = kmod * e_neg
    bg = kka * e_tot
    kg = kmod * e_tot
    if direction == 0:
        rg = r * jnp.exp(l_in)
        ymask = col <= row
    else:
        rg = r * e_ex
        ymask = earlier
    g_tot = jnp.exp(l_tot)

    for h in range(RWKV_HEADS):
        sl = slice(64 * h, 64 * h + 64)
        kkg_h, bbar_h, kbar_h, bg_h, kg_h, rg_h, v_h = (t[:, sl] for t in (kkg, bbar, kbar, bg, kg, rg, v))
        akk = jnp.where(earlier, _dg3(kkg_h, bbar_h, _NT), 0.0)
        bk = jnp.where(earlier, _dg3(kkg_h, kbar_h, _NT), 0.0)
        ark = jnp.where(ymask, _dg3(rg_h, bbar_h, _NT), 0.0)
        brk = jnp.where(ymask, _dg3(rg_h, kbar_h, _NT), 0.0)
        x = -akk
        t_inv = jnp.where(eye, 1.0, 0.0) + x
        p = x
        for _ in range(int(math.log2(C)) - 1):
            p = _dg3(p, p, _NN)
            t_inv = t_inv + _dg3(t_inv, p, _NN)
        w_m = _dg3(t_inv, kkg_h, _NN)
        u_m = _dg3(t_inv, _dg3(bk, v_h, _NN), _NN)
        m_m = jnp.where(eye, g_tot[:, sl], 0.0) - _dg3(w_m, bg_h, _TN)
        n_m = _dg3(v_h, kg_h, _TN) - _dg3(u_m, bg_h, _TN)
        q_m = rg_h - _dg3(ark, w_m, _NN)
        y_i = _dg3(brk, v_h, _NN) - _dg3(ark, u_m, _NN)
        s0 = s_ref[direction, h]
        y_ref[0, :, sl] = y_i + _dg3(q_m, s0, _NT)
        s_ref[direction, h] = _dg3(s0, m_m, _NN) + n_m


def _rwkv_scan_kernel(pcf_ref, pcb_ref, pdf_ref, pdb_ref, yf_ref, yb_ref, s_ref):
    @pl.when(pl.program_id(1) == 0)
    def _():
        s_ref[...] = jnp.zeros_like(s_ref)

    _rwkv_chunk(0, pcf_ref[0], pdf_ref[0, 0], s_ref, yf_ref)
    _rwkv_chunk(1, pcb_ref[0], pdb_ref[0, 0], s_ref, yb_ref)


def _rwkv_scan_call(pc, pd):
    B, S, _ = pc.shape
    C = RWKV_CHUNK
    nc = S // C
    return pl.pallas_call(
        _rwkv_scan_kernel,
        grid=(B, nc),
        in_specs=[
            pl.BlockSpec((1, C, 768), lambda b, c: (b, c, 0)),
            pl.BlockSpec((1, C, 768), lambda b, c: (b, nc - 1 - c, 0)),
            pl.BlockSpec((1, 1, C, 768), lambda b, c: (0, b, c, 0)),
            pl.BlockSpec((1, 1, C, 768), lambda b, c: (1, b, nc - 1 - c, 0)),
        ],
        out_specs=[
            pl.BlockSpec((1, C, 256), lambda b, c: (b, c, 0)),
            pl.BlockSpec((1, C, 256), lambda b, c: (b, nc - 1 - c, 0)),
        ],
        out_shape=[jax.ShapeDtypeStruct((B, S, 256), F32)] * 2,
        scratch_shapes=[pltpu.VMEM((2, RWKV_HEADS, HEAD_DIM, HEAD_DIM), F32)],
        compiler_params=_cparams("parallel", "arbitrary"),
        name="rwkv_scan",
    )(pc, pc, pd, pd)


def _mla_prep_kernel(zb_ref, tab_ref, qg_ref, wq_ref, kg_ref, wk_ref, wv_ref, q_ref, k_ref, v_ref):
    zb = zb_ref[0]
    tab = tab_ref[0, 0]
    cq = zb[:, 0:MLA_Q_RANK]
    ckv = zb[:, MLA_Q_RANK:MLA_Q_RANK + MLA_KV_RANK]
    kpe = zb[:, MLA_Q_RANK + MLA_KV_RANK:ZB_COLS]

    def rms(t):
        return t * lax.rsqrt(jnp.mean(t * t, axis=-1, keepdims=True) + NORM_EPS)

    cqn = (rms(cq) * qg_ref[...]).astype(BF16)
    ckvn = (rms(ckv) * kg_ref[...]).astype(BF16)
    q = _dot(cqn, wq_ref[...])
    q = _apply_rope(q, tab, 16) * ((MLA_NOPE + MLA_ROPE) ** -0.5)
    k = _dot(ckvn, wk_ref[...])
    kpe = _apply_rope(kpe, tab, 16)
    lane = lax.broadcasted_iota(jnp.int32, kpe.shape, 1)
    kpe = jnp.where((lane >= 64) & (lane < 96), kpe, 0.0)
    k = k + jnp.tile(kpe, (1, MLA_HEADS))
    q_ref[0] = q.astype(BF16)
    k_ref[0] = k.astype(BF16)
    v_ref[0] = _dot(ckvn, wv_ref[...]).astype(BF16)


def _mla_prep_call(zb, tabs, qg, wq_bf, kg, wk_bf, wv_bf):
    B, S, _ = zb.shape
    ts = min(S, 512)
    HP = MLA_HEADS * MLA_HEAD_PAD
    const = lambda shape: pl.BlockSpec(shape, lambda b, i: (0,) * len(shape))
    return pl.pallas_call(
        _mla_prep_kernel,
        grid=(B, S // ts),
        in_specs=[
            pl.BlockSpec((1, ts, ZB_COLS), lambda b, i: (b, i, 0)),
            pl.BlockSpec((1, 1, ts, 384), lambda b, i: (0, b, i, 0)),
            const((1, MLA_Q_RANK)), const((MLA_Q_RANK, HP)),
            const((1, MLA_KV_RANK)), const((MLA_KV_RANK, HP)), const((MLA_KV_RANK, MLA_WIDTH)),
        ],
        out_specs=[
            pl.BlockSpec((1, ts, HP), lambda b, i: (b, i, 0)),
            pl.BlockSpec((1, ts, HP), lambda b, i: (b, i, 0)),
            pl.BlockSpec((1, ts, MLA_WIDTH), lambda b, i: (b, i, 0)),
        ],
        out_shape=[
            jax.ShapeDtypeStruct((B, S, HP), BF16),
            jax.ShapeDtypeStruct((B, S, HP), BF16),
            jax.ShapeDtypeStruct((B, S, MLA_WIDTH), BF16),
        ],
        compiler_params=_cparams("parallel", "parallel"),
        name="mla_prep",
    )(zb, tabs, qg, wq_bf, kg, wk_bf, wv_bf)


def _mla_attn_kernel(q_ref, k_ref, v_ref, o_ref, *, tk, nk):
    tq = q_ref.shape[1]
    outs = []
    for hh in range(2):
        q = q_ref[0, :, 128 * hh:128 * hh + 128]

        def body(i, carry):
            m, l, acc = carry
            ks = pl.multiple_of(i * tk, tk)
            kt = k_ref[0, pl.ds(ks, tk), 128 * hh:128 * hh + 128]
            vt = v_ref[0, pl.ds(ks, tk), :]
            s = lax.dot_general(q, kt, (_NT, ((), ())), preferred_element_type=F32)
            m_new = jnp.maximum(m, jnp.max(s, axis=-1, keepdims=True))
            alpha = jnp.exp(m - m_new)
            p = jnp.exp(s - m_new)
            l = alpha * l + jnp.sum(p, axis=-1, keepdims=True)
            acc = alpha * acc + _dot(p.astype(BF16), vt)
            return m_new, l, acc

        init = (jnp.full((tq, 1), -jnp.inf, F32), jnp.zeros((tq, 1), F32), jnp.zeros((tq, 128), F32))
        m, l, acc = lax.fori_loop(0, nk, body, init)
        outs.append(acc / l)
    lane = lax.broadcasted_iota(jnp.int32, (tq, 128), 1)
    o_ref[0] = jnp.where(lane < 64, outs[0], outs[1])


def _mla_attn_call(q, k, v):
    B, S, _ = q.shape
    tq = min(S, 512)
    tk = min(S, 512)
    return pl.pallas_call(
        functools.partial(_mla_attn_kernel, tk=tk, nk=S // tk),
        grid=(B, MLA_HEADS // 2, S // tq),
        in_specs=[
            pl.BlockSpec((1, tq, 256), lambda b, h, i: (b, i, h)),
            pl.BlockSpec((1, S, 256), lambda b, h, i: (b, 0, h)),
            pl.BlockSpec((1, S, 128), lambda b, h, i: (b, 0, h)),
        ],
        out_specs=pl.BlockSpec((1, tq, 128), lambda b, h, i: (b, i, h)),
        out_shape=jax.ShapeDtypeStruct((B, S, MLA_WIDTH), F32),
        compiler_params=_cparams("parallel", "parallel", "arbitrary"),
        name="mla_attn",
    )(q, k, v)


def _ret_dir(direction, zc, tab, lrv, lrh, r_ref, y_ref):
    C = RET_CHUNK
    q = _apply_rope(zc[:, 0:256], tab, 32)
    k = _apply_rope(zc[:, 256:512], tab, 32) * (HEAD_DIM ** -0.5)
    v = zc[:, 512:768]
    lgv = -jnp.exp(lrv)
    lgh = -jnp.exp(lrh)
    pos = lax.broadcasted_iota(jnp.int32, (C, 1), 0).astype(F32)
    row = lax.broadcasted_iota(jnp.int32, (C, C), 0)
    col = lax.broadcasted_iota(jnp.int32, (C, C), 1)
    if direction == 0:
        kw = k * jnp.exp(lgv * (C - 1.0 - pos))
        qw = q * jnp.exp(lgv * (pos + 1.0))
        mask = col <= row
        dist = (row - col).astype(F32)
    else:
        kw = k * jnp.exp(lgv * pos)
        qw = q * jnp.exp(lgv * (C - pos))
        mask = col > row
        dist = (col - row).astype(F32)
    dist = jnp.maximum(dist, 0.0)
    qb, kb, vb, kwb, qwb = (t.astype(BF16) for t in (q, k, v, kw, qw))
    for h in range(RET_HEADS):
        sl = slice(64 * h, 64 * h + 64)
        lg = lgh[h:h + 1, :]
        dmat = jnp.where(mask, jnp.exp(lg * dist), 0.0)
        sc = lax.dot_general(qb[:, sl], kb[:, sl], (_NT, ((), ())), preferred_element_type=F32) * dmat
        inner = _dot(sc.astype(BF16), vb[:, sl])
        r0 = r_ref[direction, h]
        cross = _dot(qwb[:, sl], r0.astype(BF16))
        y_ref[0, :, sl] = inner + cross
        kv = lax.dot_general(kwb[:, sl], vb[:, sl], (_TN, ((), ())), preferred_element_type=F32)
        r_ref[direction, h] = r0 * jnp.exp(lg[:, 0:64] * C) + kv


def _ret_kernel(zf_ref, zb_ref, tf_ref, tb_ref, lrv_ref, lrh_ref, yf_ref, yb_ref, r_ref):
    @pl.when(pl.program_id(1) == 0)
    def _():
        r_ref[...] = jnp.zeros_like(r_ref)

    _ret_dir(0, zf_ref[0], tf_ref[0, 0], lrv_ref[0:1, :], lrh_ref[0:4, :], r_ref, yf_ref)
    _ret_dir(1, zb_ref[0], tb_ref[0, 0], lrv_ref[1:2, :], lrh_ref[4:8, :], r_ref, yb_ref)


def _ret_call(zc, tabs, lr_vec, lr_heads):
    B, S, _ = zc.shape
    C = RET_CHUNK
    nc = S // C
    const = lambda shape: pl.BlockSpec(shape, lambda b, c: (0,) * len(shape))
    return pl.pallas_call(
        _ret_kernel,
        grid=(B, nc),
        in_specs=[
            pl.BlockSpec((1, C, ZC_COLS), lambda b, c: (b, c, 0)),
            pl.BlockSpec((1, C, ZC_COLS), lambda b, c: (b, nc - 1 - c, 0)),
            pl.BlockSpec((1, 1, C, 384), lambda b, c: (1, b, c, 0)),
            pl.BlockSpec((1, 1, C, 384), lambda b, c: (1, b, nc - 1 - c, 0)),
            const((2, 256)), const((8, 128)),
        ],
        out_specs=[
            pl.BlockSpec((1, C, 256), lambda b, c: (b, c, 0)),
            pl.BlockSpec((1, C, 256), lambda b, c: (b, nc - 1 - c, 0)),
        ],
        out_shape=[jax.ShapeDtypeStruct((B, S, 256), F32)] * 2,
        scratch_shapes=[pltpu.VMEM((2, RET_HEADS, HEAD_DIM, HEAD_DIM), F32)],
        compiler_params=_cparams("parallel", "arbitrary"),
        name="retention",
    )(zc, zc, tabs, tabs, lr_vec, lr_heads)


def _head_norm(y, avg_bd, eps):
    mu = _dot_hi(y, avg_bd)
    d = y - mu
    var = _dot_hi(d * d, avg_bd)
    return d * lax.rsqrt(var + eps)


def _outproj_kernel(x_ref, g2_ref, yf_ref, yb_ref, pg_ref, lng_ref, lnb_ref, om_ref, rf_ref, rb_ref,
                    gate_ref, gng_ref, avg_ref, wa_ref, wb_ref, wc_ref, o_ref):
    avg_bd = avg_ref[...]
    y = _head_norm(yf_ref[0] + yb_ref[0], avg_bd, RWKV_LN_EPS) * lng_ref[...] + lnb_ref[...]
    pg = pg_ref[0]
    o_a = (y + pg[:, 256:512]) * pg[:, 0:256]
    yr = _head_norm(rf_ref[0] + rb_ref[0], avg_bd, RET_LN_EPS) * gng_ref[...]
    gate = gate_ref[0]
    o_c = (gate * _sigmoid(gate)) * yr
    mixed = _dot(o_a.astype(BF16), wa_ref[...])
    mixed = mixed + _dot(om_ref[0].astype(BF16), wb_ref[...])
    mixed = mixed + _dot(o_c.astype(BF16), wc_ref[...])
    o_ref[0] = x_ref[0] + g2_ref[0] * mixed


def _outproj_call(x, g2, yf, yb, pg, ln_g, ln_b, o_mla, rf, rb, zc, gn_g, avg_bd, wa, wb, wc):
    B, S, D = x.shape
    tm = min(S, 512)
    tok = lambda n: pl.BlockSpec((1, tm, n), lambda b, i: (b, i, 0))
    const = lambda shape: pl.BlockSpec(shape, lambda b, i: (0,) * len(shape))
    return pl.pallas_call(
        _outproj_kernel,
        grid=(B, S // tm),
        in_specs=[
            tok(D),
            pl.BlockSpec((1, 1, D), lambda b, i: (b, 0, 0)),
            tok(256), tok(256), tok(512), const((1, 256)), const((1, 256)),
            tok(512), tok(256), tok(256),
            pl.BlockSpec((1, tm, 256), lambda b, i: (b, i, 3)),
            const((1, 256)), const((256, 256)),
            const((256, D)), const((512, D)), const((256, D)),
        ],
        out_specs=tok(D),
        out_shape=jax.ShapeDtypeStruct((B, S, D), F32),
        compiler_params=_cparams("parallel", "parallel"),
        name="outproj",
    )(x, g2, yf, yb, pg, ln_g, ln_b, o_mla, rf, rb, zc, gn_g, avg_bd, wa, wb, wc)


def _block_diag2(w):
    z = jnp.zeros_like(w[0])
    return jnp.concatenate([jnp.concatenate([w[0], z], axis=1), jnp.concatenate([z, w[1]], axis=1)], axis=0)


def _pad_inproj(w):
    D = w.shape[0]
    o = RWKV_COLS + MLA_Q_RANK + MLA_KV_RANK
    return jnp.concatenate(
        [w[:, :o], jnp.zeros((D, 64), w.dtype), w[:, o:o + MLA_ROPE], jnp.zeros((D, 32), w.dtype), w[:, o + MLA_ROPE:]],
        axis=1)


def _pad_wq(w):
    r = w.shape[0]
    w = w.reshape(r, MLA_HEADS, MLA_NOPE + MLA_ROPE)
    w = jnp.concatenate([w, jnp.zeros((r, MLA_HEADS, 32), w.dtype)], axis=2)
    return w.reshape(r, MLA_HEADS * MLA_HEAD_PAD)


def _split_wkv(w):
    r = w.shape[0]
    w = w.reshape(r, MLA_HEADS, MLA_NOPE + MLA_V)
    wk = jnp.concatenate([w[:, :, :MLA_NOPE], jnp.zeros((r, MLA_HEADS, 64), w.dtype)], axis=2)
    return wk.reshape(r, MLA_HEADS * MLA_HEAD_PAD), w[:, :, MLA_NOPE:].reshape(r, MLA_WIDTH)


def kernel(x, c, positions, w_ada, b_ada, w_ff1_in, w_ff1_out, w_ff2_in, w_ff2_out, w_in, w_out, rwkv_mu, rwkv_w0,
           rwkv_w_up, rwkv_a0, rwkv_a_up, rwkv_g_up, rwkv_k_k, rwkv_k_a, rwkv_r_k, rwkv_ln_g, rwkv_ln_b,
           mla_q_norm_g, mla_w_uq, mla_kv_norm_g, mla_w_ukv, ret_log_rate, ret_gn_g, final_norm_g):
    B, S, D = x.shape
    L = w_ada.shape[0]
    mod = _mod_call(c, w_ada, b_ada)
    tabs = _rope_call(positions)

    head_id = jnp.arange(256) // HEAD_DIM
    same_head = (head_id[:, None] == head_id[None, :]).astype(F32)
    ones_bd = same_head
    avg_bd = same_head / HEAD_DIM
    final_g = final_norm_g.reshape(1, D)

    for l in range(L):
        m = [mod[l, :, i * D:(i + 1) * D].reshape(B, 1, D) for i in range(N_MOD)]
        sh1, sc1, g1, sh2, sc2, g2, sh3, sc3, g3 = m

        x = _ffn_call(x, sh1, sc1, g1, w_ff1_in[l].astype(BF16), w_ff1_out[l].astype(BF16), final_g, False)

        za, zb, zc = _inproj_call(x, sh2, sc2, _pad_inproj(w_in[l]).astype(BF16))

        pc, pd, pg = _rwkv_prep_call(
            za, rwkv_mu[l].reshape(1, -1), rwkv_w0[l].reshape(1, 512), _block_diag2(rwkv_w_up[l]),
            rwkv_a0[l].reshape(1, 512), _block_diag2(rwkv_a_up[l]), rwkv_g_up[l],
            rwkv_k_k[l].reshape(1, 256), rwkv_k_a[l].reshape(1, 256), rwkv_r_k[l].reshape(1, 256), ones_bd)
        yf, yb = _rwkv_scan_call(pc, pd)

        wk, wv = _split_wkv(mla_w_ukv[l])
        q, k, v = _mla_prep_call(zb, tabs, mla_q_norm_g[l].reshape(1, -1), _pad_wq(mla_w_uq[l]).astype(BF16),
                                 mla_kv_norm_g[l].reshape(1, -1), wk.astype(BF16), wv.astype(BF16))
        o_mla = _mla_attn_call(q, k, v)

        lr = ret_log_rate[l]
        lr_vec = jnp.repeat(lr, HEAD_DIM, axis=1)
        lr_heads = jnp.broadcast_to(lr.reshape(8, 1), (8, 128))
        rf, rb = _ret_call(zc, tabs, lr_vec, lr_heads)

        wo = w_out[l].astype(BF16)
        x = _outproj_call(x, g2, yf, yb, pg, rwkv_ln_g[l].reshape(1, 256), rwkv_ln_b[l].reshape(1, 256), o_mla,
                          rf, rb, zc, ret_gn_g[l].reshape(1, 256), avg_bd,
                          wo[0:256], wo[256:768], wo[768:1024])

        x = _ffn_call(x, sh3, sc3, g3, w_ff2_in[l].astype(BF16), w_ff2_out[l].astype(BF16), final_g, l == L - 1)
    return x
```

```python
import functools
import math

import jax
import jax.numpy as jnp
from jax import lax
from jax.experimental import pallas as pl
from jax.experimental.pallas import tpu as pltpu

F32 = jnp.float32
BF16 = jnp.bfloat16
HIGHEST = lax.Precision.HIGHEST

D_MODEL = 1024
HEAD_DIM = 64
RWKV_WIDTH = 256
RWKV_HEADS = 4
RWKV_LORA = 64
RWKV_GATE_LORA = 128
RWKV_COLS = 1152
MLA_HEADS = 8
MLA_NOPE = 64
MLA_ROPE = 32
MLA_V = 64
MLA_Q_RANK = 384
MLA_KV_RANK = 256
MLA_WIDTH = 512
MLA_HEAD_PAD = 128
MLA_TQ = 512
MLA_TK = 512
RET_WIDTH = 256
RET_HEADS = 4
RET_CHUNK = 128
RWKV_CHUNK = 64
RWKV_PASSES = 1
D_FF = 2816
N_MOD = 9
ROPE_BASE = 10000.0
NORM_EPS = 1e-6
RWKV_LN_EPS = 64e-5
RET_LN_EPS = 1e-5

VMEM_LIMIT = 48 * 1024 * 1024


def _cparams(*sem):
    return pltpu.CompilerParams(dimension_semantics=sem, vmem_limit_bytes=VMEM_LIMIT)


def _sigmoid(x):
    return 1.0 / (1.0 + jnp.exp(-x))


def _dot(a, b):
    return jnp.dot(a, b, preferred_element_type=F32)


def _dot_hi(a, b):
    return jnp.dot(a, b, precision=HIGHEST, preferred_element_type=F32)


def _split(a):
    hi = a.astype(BF16)
    lo = (a - hi.astype(F32)).astype(BF16)
    return hi, lo


def _dg3(a, b, dims):
    ah, al = _split(a)
    bh, bl = _split(b)
    dn = (dims, ((), ()))
    out = lax.dot_general(ah, bh, dn, preferred_element_type=F32)
    out = out + lax.dot_general(ah, bl, dn, preferred_element_type=F32)
    out = out + lax.dot_general(al, bh, dn, preferred_element_type=F32)
    return out


_NN = ((1,), (0,))
_NT = ((1,), (1,))
_TN = ((0,), (0,))


def _rms_mod(x, sh, sc):
    ms = jnp.mean(x * x, axis=-1, keepdims=True)
    return (x * lax.rsqrt(ms + NORM_EPS)) * (1.0 + sc) + sh


def _mod_kernel(c_ref, w_ref, b_ref, o_ref):
    c = c_ref[...]
    cond = c * _sigmoid(c)
    o_ref[0] = _dot_hi(cond, w_ref[0]) + b_ref[0]


def _mod_call(c, w_ada, b_ada):
    L, D, N = w_ada.shape
    B = c.shape[0]
    tn = 1152
    return pl.pallas_call(
        _mod_kernel,
        grid=(L, N // tn),
        in_specs=[
            pl.BlockSpec((B, D), lambda l, j: (0, 0)),
            pl.BlockSpec((1, D, tn), lambda l, j: (l, 0, j)),
            pl.BlockSpec((1, 1, tn), lambda l, j: (l, 0, j)),
        ],
        out_specs=pl.BlockSpec((1, B, tn), lambda l, j: (l, 0, j)),
        out_shape=jax.ShapeDtypeStruct((L, B, N), F32),
        compiler_params=_cparams("parallel", "parallel"),
        name="adaln_mod",
    )(c, w_ada, b_ada.reshape(L, 1, N))


def _rope_kernel(pos_ref, inv_ref, mc_ref, m1_ref, ma_ref, mb_ref, o_ref):
    pos = pos_ref[0].astype(F32)
    for i in range(2):
        ang = pos * inv_ref[i:i + 1, :]
        cos = jnp.cos(ang)
        sin = jnp.sin(ang)
        o_ref[i, 0, :, 0:128] = cos * mc_ref[i:i + 1, :] + m1_ref[i:i + 1, :]
        o_ref[i, 0, :, 128:256] = sin * ma_ref[i:i + 1, :]
        o_ref[i, 0, :, 256:384] = sin * mb_ref[i:i + 1, :]


def _rope_consts():
    lane = jnp.arange(128)
    inv_m = ROPE_BASE ** (-jnp.arange(0, MLA_ROPE, 2, dtype=F32) / MLA_ROPE)
    in_rope = (lane >= 64) & (lane < 96)
    inv0 = jnp.where(in_rope, inv_m[(lane - 64) % 16], 0.0)
    mc0 = in_rope.astype(F32)
    m10 = (lane < 64).astype(F32)
    ma0 = jnp.where((lane >= 64) & (lane < 80), -1.0, 0.0)
    mb0 = jnp.where((lane >= 80) & (lane < 96), 1.0, 0.0)
    inv_r = ROPE_BASE ** (-jnp.arange(0, HEAD_DIM, 2, dtype=F32) / HEAD_DIM)
    inv1 = inv_r[lane % 32]
    mc1 = jnp.ones((128,), F32)
    m11 = jnp.zeros((128,), F32)
    ma1 = jnp.where((lane % 64) < 32, -1.0, 0.0)
    mb1 = jnp.where((lane % 64) >= 32, 1.0, 0.0)
    st = lambda a, b: jnp.stack([a, b]).astype(F32)
    return st(inv0, inv1), st(mc0, mc1), st(m10, m11), st(ma0, ma1), st(mb0, mb1)


def _rope_call(positions):
    B, S = positions.shape
    ts = min(S, 512)
    consts = _rope_consts()
    cspec = pl.BlockSpec((2, 128), lambda b, i: (0, 0))
    return pl.pallas_call(
        _rope_kernel,
        grid=(B, S // ts),
        in_specs=[pl.BlockSpec((1, ts, 1), lambda b, i: (b, i, 0))] + [cspec] * 5,
        out_specs=pl.BlockSpec((2, 1, ts, 384), lambda b, i: (0, b, i, 0)),
        out_shape=jax.ShapeDtypeStruct((2, B, S, 384), F32),
        compiler_params=_cparams("parallel", "parallel"),
        name="rope_tables",
    )(positions.reshape(B, S, 1), *consts)


def _apply_rope(x, tab, shift):
    n = x.shape[1] // 128
    width = x.shape[1]
    cc = jnp.tile(tab[:, 0:128], (1, n))
    sa = jnp.tile(tab[:, 128:256], (1, n))
    sb = jnp.tile(tab[:, 256:384], (1, n))
    return x * cc + pltpu.roll(x, width - shift, axis=1) * sa + pltpu.roll(x, shift, axis=1) * sb


def _ffn_kernel(x_ref, sh_ref, sc_ref, g_ref, wg_ref, wu_ref, wo_ref, fg_ref, o_ref, h_scr, acc_scr, *, nj, final):
    j = pl.program_id(2)

    @pl.when(j == 0)
    def _():
        h = _rms_mod(x_ref[0], sh_ref[0], sc_ref[0])
        h_scr[...] = h.astype(BF16)
        acc_scr[...] = jnp.zeros_like(acc_scr)

    h = h_scr[...]
    gate = _dot(h, wg_ref[...])
    up = _dot(h, wu_ref[...])
    act = (gate * _sigmoid(gate) * up).astype(BF16)
    acc_scr[...] += _dot(act, wo_ref[...])

    @pl.when(j == nj - 1)
    def _():
        y = x_ref[0] + (0.5 * g_ref[0]) * acc_scr[...]
        if final:
            ms = jnp.mean(y * y, axis=-1, keepdims=True)
            y = (y * lax.rsqrt(ms + NORM_EPS)) * fg_ref[...]
        o_ref[0] = y


def _ffn_call(x, sh, sc, g, w_in_bf, w_out_bf, final_g, final):
    B, S, D = x.shape
    tm = min(S, 512)
    tf = 1408
    nj = D_FF // tf
    vec = pl.BlockSpec((1, 1, D), lambda b, i, j: (b, 0, 0))
    return pl.pallas_call(
        functools.partial(_ffn_kernel, nj=nj, final=final),
        grid=(B, S // tm, nj),
        in_specs=[
            pl.BlockSpec((1, tm, D), lambda b, i, j: (b, i, 0)),
            vec, vec, vec,
            pl.BlockSpec((D, tf), lambda b, i, j: (0, j)),
            pl.BlockSpec((D, tf), lambda b, i, j: (0, j + nj)),
            pl.BlockSpec((tf, D), lambda b, i, j: (j, 0)),
            pl.BlockSpec((1, D), lambda b, i, j: (0, 0)),
        ],
        out_specs=pl.BlockSpec((1, tm, D), lambda b, i, j: (b, i, 0)),
        out_shape=jax.ShapeDtypeStruct((B, S, D), F32),
        scratch_shapes=[pltpu.VMEM((tm, D), BF16), pltpu.VMEM((tm, D), F32)],
        compiler_params=_cparams("parallel", "parallel", "arbitrary"),
        name="ffn",
    )(x, sh, sc, g, w_in_bf, w_in_bf, w_out_bf, final_g)


ZB_COLS = MLA_Q_RANK + MLA_KV_RANK + MLA_HEAD_PAD
ZC_COLS = 4 * RET_WIDTH
Z_COLS = RWKV_COLS + ZB_COLS + ZC_COLS


def _inproj_kernel(x_ref, sh_ref, sc_ref, w_ref, za_ref, zb_ref, zc_ref):
    h = _rms_mod(x_ref[0], sh_ref[0], sc_ref[0]).astype(BF16)
    z = _dot(h, w_ref[...])
    za_ref[0] = z[:, 0:RWKV_COLS]
    zb_ref[0] = z[:, RWKV_COLS:RWKV_COLS + ZB_COLS]
    zc_ref[0] = z[:, RWKV_COLS + ZB_COLS:Z_COLS]


def _inproj_call(x, sh, sc, wz_bf):
    B, S, D = x.shape
    tm = min(S, 512)
    vec = pl.BlockSpec((1, 1, D), lambda b, i: (b, 0, 0))
    out = lambda n: pl.BlockSpec((1, tm, n), lambda b, i: (b, i, 0))
    return pl.pallas_call(
        _inproj_kernel,
        grid=(B, S // tm),
        in_specs=[
            pl.BlockSpec((1, tm, D), lambda b, i: (b, i, 0)),
            vec, vec,
            pl.BlockSpec((D, Z_COLS), lambda b, i: (0, 0)),
        ],
        out_specs=[out(RWKV_COLS), out(ZB_COLS), out(ZC_COLS)],
        out_shape=[jax.ShapeDtypeStruct((B, S, n), F32) for n in (RWKV_COLS, ZB_COLS, ZC_COLS)],
        compiler_params=_cparams("parallel", "parallel"),
        name="inproj",
    )(x, sh, sc, wz_bf)


def _rwkv_prep_kernel(z_ref, zp_ref, zn_ref, mu_ref, w0_ref, wup_ref, a0_ref, aup_ref, gup_ref,
                      kk_ref, ka_ref, rk_ref, ones_ref, pc_ref, pd_ref, pg_ref, *, ts, nt):
    i = pl.program_id(1)
    z = z_ref[0]
    row = lax.broadcasted_iota(jnp.int32, (ts, 1), 0)
    prev_edge = jnp.where(i > 0, zp_ref[0, 7:8, :], 0.0)
    next_edge = jnp.where(i < nt - 1, zn_ref[0, 0:1, :], 0.0)
    prev = jnp.where(row == 0, prev_edge, pltpu.roll(z, 1, axis=0))
    nxt = jnp.where(row == ts - 1, next_edge, pltpu.roll(z, ts - 1, axis=0))
    zs = z + mu_ref[...] * (0.5 * (prev + nxt) - z)

    r = zs[:, 0:256]
    k = zs[:, 256:512]
    v = zs[:, 512:768]
    g_lo = zs[:, 768:896]
    w_lo = zs[:, 896:1024]
    a_lo = zs[:, 1024:1152]

    w_raw = w0_ref[...] + _dg3(jnp.tanh(w_lo), wup_ref[...], _NN)
    logw = (-math.exp(-0.5)) * _sigmoid(w_raw)
    a = _sigmoid(a0_ref[...] + _dg3(a_lo, aup_ref[...], _NN))
    g = _dg3(_sigmoid(g_lo), gup_ref[...], _NN)

    ones_bd = ones_ref[...]
    kk0 = k * kk_ref[...]
    nrm = jnp.sqrt(_dot_hi(kk0 * kk0, ones_bd))
    kk = kk0 / jnp.maximum(nrm, 1e-12)
    bonus = _dot_hi(r * k * rk_ref[...], ones_bd) * v

    pc_ref[0, :, 0:256] = r
    pc_ref[0, :, 256:512] = v
    pc_ref[0, :, 512:768] = kk
    for d in range(2):
        a_d = a[:, 256 * d:256 * d + 256]
        pd_ref[d, 0, :, 0:256] = k * (1.0 + (a_d - 1.0) * ka_ref[...])
        pd_ref[d, 0, :, 256:512] = kk * a_d
        pd_ref[d, 0, :, 512:768] = logw[:, 256 * d:256 * d + 256]
    pg_ref[0, :, 0:256] = g
    pg_ref[0, :, 256:512] = bonus


def _rwkv_prep_call(za, mu, w0, wup_bd, a0, aup_bd, gup, k_k, k_a, r_k, ones_bd):
    B, S, _ = za.shape
    ts = min(S, 512)
    nt = S // ts
    nb8 = S // 8
    const = lambda shape: pl.BlockSpec(shape, lambda b, i: (0,) * len(shape))
    return pl.pallas_call(
        functools.partial(_rwkv_prep_kernel, ts=ts, nt=nt),
        grid=(B, nt),
        in_specs=[
            pl.BlockSpec((1, ts, RWKV_COLS), lambda b, i: (b, i, 0)),
            pl.BlockSpec((1, 8, RWKV_COLS), lambda b, i: (b, jnp.maximum(i * (ts // 8) - 1, 0), 0)),
            pl.BlockSpec((1, 8, RWKV_COLS), lambda b, i: (b, jnp.minimum((i + 1) * (ts // 8), nb8 - 1), 0)),
            const((1, RWKV_COLS)),
            const((1, 512)), const((128, 512)), const((1, 512)), const((128, 512)), const((128, 256)),
            const((1, 256)), const((1, 256)), const((1, 256)), const((256, 256)),
        ],
        out_specs=[
            pl.BlockSpec((1, ts, 768), lambda b, i: (b, i, 0)),
            pl.BlockSpec((2, 1, ts, 768), lambda b, i: (0, b, i, 0)),
            pl.BlockSpec((1, ts, 512), lambda b, i: (b, i, 0)),
        ],
        out_shape=[
            jax.ShapeDtypeStruct((B, S, 768), F32),
            jax.ShapeDtypeStruct((2, B, S, 768), F32),
            jax.ShapeDtypeStruct((B, S, 512), F32),
        ],
        compiler_params=_cparams("parallel", "parallel"),
        name="rwkv_prep",
    )(za, za, za, mu, w0, wup_bd, a0, aup_bd, gup, k_k, k_a, r_k, ones_bd)


def _rwkv_dir_operands(direction, pc, pd):
    C = RWKV_CHUNK
    r = pc[:, 0:256]
    v = pc[:, 256:512]
    kk = pc[:, 512:768]
    kmod = pd[:, 0:256]
    kka = pd[:, 256:512]
    logw = pd[:, 512:768]

    row = lax.broadcasted_iota(jnp.int32, (C, C), 0)
    col = lax.broadcasted_iota(jnp.int32, (C, C), 1)
    tri = ((col <= row) if direction == 0 else (col >= row)).astype(F32)
    l_in = _dot_hi(tri, logw)
    l_ex = l_in - logw
    l_tot = l_in[C - 1:C, :] if direction == 0 else l_in[0:1, :]
    e_ex = jnp.exp(l_ex)
    e_neg = jnp.exp(-l_in)
    e_tot = jnp.exp(l_tot - l_in)
    kkg = kk * e_ex
    bbar = kka * e_neg
    kbar = kmod * e_neg
    bg = kka * e_tot
    kg = kmod * e_tot
    rg = r * jnp.exp(l_in) if direction == 0 else r * e_ex
    g_tot = jnp.exp(l_tot)
    return dict(kkg=kkg, bbar=bbar, kbar=kbar, bg=bg, kg=kg, rg=rg, v=v, g_tot=g_tot)


def _operand(a, passes):
    hi = a.astype(BF16)
    lo = (a - hi.astype(F32)).astype(BF16) if passes > 1 else None
    return hi, lo


def _mm(a, b, dims):
    dn = (dims, ((), ()))
    out = lax.dot_general(a[0], b[0], dn, preferred_element_type=F32)
    if b[1] is not None:
        out = out + lax.dot_general(a[0], b[1], dn, preferred_element_type=F32)
    if a[1] is not None:
        out = out + lax.dot_general(a[1], b[0], dn, preferred_element_type=F32)
    return out


def _rwkv_scan_kernel(pcf_ref, pcb_ref, pdf_ref, pdb_ref, yf_ref, yb_ref, s_ref):
    C = RWKV_CHUNK
    P = RWKV_PASSES

    @pl.when(pl.program_id(1) == 0)
    def _():
        s_ref[...] = jnp.zeros_like(s_ref)

    row = lax.broadcasted_iota(jnp.int32, (C, C), 0)
    col = lax.broadcasted_iota(jnp.int32, (C, C), 1)
    eye = col == row
    ident = jnp.where(eye, 1.0, 0.0)
    ops = (_rwkv_dir_operands(0, pcf_ref[0], pdf_ref[0, 0]), _rwkv_dir_operands(1, pcb_ref[0], pdb_ref[0, 0]))
    earlier = (col < row, col > row)
    ymask = (col <= row, col > row)
    units = [(d, h) for d in range(2) for h in range(RWKV_HEADS)]
    nu = len(units)
    sl = lambda h: slice(64 * h, 64 * h + 64)
    s0 = [s_ref[d, h] for d, h in units]

    def head(name):
        return [_operand(ops[d][name][:, sl(h)], P) for d, h in units]

    kkg, bbar, kbar, bg, kg, rg, v = (head(n) for n in ("kkg", "bbar", "kbar", "bg", "kg", "rg", "v"))
    akk = [jnp.where(earlier[d], _mm(kkg[i], bbar[i], _NT), 0.0) for i, (d, h) in enumerate(units)]
    bk = [jnp.where(earlier[d], _mm(kkg[i], kbar[i], _NT), 0.0) for i, (d, h) in enumerate(units)]
    ark = [jnp.where(ymask[d], _mm(rg[i], bbar[i], _NT), 0.0) for i, (d, h) in enumerate(units)]
    brk = [jnp.where(ymask[d], _mm(rg[i], kbar[i], _NT), 0.0) for i, (d, h) in enumerate(units)]
    pw = [-a for a in akk]
    t_inv = [ident + x for x in pw]
    for _ in range(int(math.log2(C)) - 1):
        pw_o = [_operand(x, P) for x in pw]
        pw = [_mm(x, x, _NN) for x in pw_o]
        pw_o = [_operand(x, P) for x in pw]
        t_inv = [t + _mm(_operand(t, P), x, _NN) for t, x in zip(t_inv, pw_o)]
    t_o = [_operand(t, P) for t in t_inv]
    bkv = [_mm(_operand(b, P), x, _NN) for b, x in zip(bk, v)]
    w_m = [_mm(t, x, _NN) for t, x in zip(t_o, kkg)]
    u_m = [_mm(t, _operand(x, P), _NN) for t, x in zip(t_o, bkv)]
    w_o = [_operand(x, P) for x in w_m]
    u_o = [_operand(x, P) for x in u_m]
    ark_o = [_operand(x, P) for x in ark]
    brk_o = [_operand(x, P) for x in brk]
    m_m = [jnp.where(eye, ops[d]["g_tot"][:, sl(h)], 0.0) - _mm(w_o[i], bg[i], _TN) for i, (d, h) in enumerate(units)]
    n_m = [_mm(v[i], kg[i], _TN) - _mm(u_o[i], bg[i], _TN) for i in range(nu)]
    q_m = [ops[d]["rg"][:, sl(h)] - _mm(ark_o[i], w_o[i], _NN) for i, (d, h) in enumerate(units)]
    y_i = [_mm(brk_o[i], v[i], _NN) - _mm(ark_o[i], u_o[i], _NN) for i in range(nu)]
    s_o = [_operand(s, P) for s in s0]
    y = [y_i[i] + _mm(_operand(q_m[i], P), s_o[i], _NT) for i in range(nu)]
    s_new = [_mm(s_o[i], _operand(m_m[i], P), _NN) + n_m[i] for i in range(nu)]
    for i, (d, h) in enumerate(units):
        (yf_ref if d == 0 else yb_ref)[0, :, sl(h)] = y[i]
        s_ref[d, h] = s_new[i]


def _rwkv_scan_call(pc, pd):
    B, S, _ = pc.shape
    C = RWKV_CHUNK
    nc = S // C
    return pl.pallas_call(
        _rwkv_scan_kernel,
        grid=(B, nc),
        in_specs=[
            pl.BlockSpec((1, C, 768), lambda b, c: (b, c, 0)),
            pl.BlockSpec((1, C, 768), lambda b, c: (b, nc - 1 - c, 0)),
            pl.BlockSpec((1, 1, C, 768), lambda b, c: (0, b, c, 0)),
            pl.BlockSpec((1, 1, C, 768), lambda b, c: (1, b, nc - 1 - c, 0)),
        ],
        out_specs=[
            pl.BlockSpec((1, C, 256), lambda b, c: (b, c, 0)),
            pl.BlockSpec((1, C, 256), lambda b, c: (b, nc - 1 - c, 0)),
        ],
        out_shape=[jax.ShapeDtypeStruct((B, S, 256), F32)] * 2,
        scratch_shapes=[pltpu.VMEM((2, RWKV_HEADS, HEAD_DIM, HEAD_DIM), F32)],
        compiler_params=_cparams("parallel", "arbitrary"),
        name="rwkv_scan",
    )(pc, pc, pd, pd)


def _mla_prep_kernel(zb_ref, tab_ref, qg_ref, wq_ref, kg_ref, wk_ref, wvt_ref, qt_ref, k_ref, vt_ref):
    zb = zb_ref[0]
    tab = tab_ref[0, 0]
    cq = zb[:, 0:MLA_Q_RANK]
    ckv = zb[:, MLA_Q_RANK:MLA_Q_RANK + MLA_KV_RANK]
    kpe = zb[:, MLA_Q_RANK + MLA_KV_RANK:ZB_COLS]

    def rms(t):
        return t * lax.rsqrt(jnp.mean(t * t, axis=-1, keepdims=True) + NORM_EPS)

    cqn = (rms(cq) * qg_ref[...]).astype(BF16)
    ckvn = (rms(ckv) * kg_ref[...]).astype(BF16)
    q = _dot(cqn, wq_ref[...])
    q = _apply_rope(q, tab, 16) * (math.log2(math.e) * (MLA_NOPE + MLA_ROPE) ** -0.5)
    k = _dot(ckvn, wk_ref[...])
    kpe = _apply_rope(kpe, tab, 16)
    k = k + jnp.tile(kpe, (1, MLA_HEADS))
    qt_ref[0] = q.T.astype(BF16)
    k_ref[0] = k.astype(BF16)
    vt_ref[0, 0] = lax.dot_general(wvt_ref[...], ckvn, (_NT, ((), ())), preferred_element_type=F32).astype(BF16)


def _mla_key_tile(S):
    return min(S, MLA_TK)


def _mla_prep_call(zb, tabs, qg, wq_bf, kg, wk_bf, wvt_bf):
    B, S, _ = zb.shape
    ts = _mla_key_tile(S)
    HP = MLA_HEADS * MLA_HEAD_PAD
    const = lambda shape: pl.BlockSpec(shape, lambda b, i: (0,) * len(shape))
    return pl.pallas_call(
        _mla_prep_kernel,
        grid=(B, S // ts),
        in_specs=[
            pl.BlockSpec((1, ts, ZB_COLS), lambda b, i: (b, i, 0)),
            pl.BlockSpec((1, 1, ts, 384), lambda b, i: (0, b, i, 0)),
            const((1, MLA_Q_RANK)), const((MLA_Q_RANK, HP)),
            const((1, MLA_KV_RANK)), const((MLA_KV_RANK, HP)), const((MLA_WIDTH, MLA_KV_RANK)),
        ],
        out_specs=[
            pl.BlockSpec((1, HP, ts), lambda b, i: (b, 0, i)),
            pl.BlockSpec((1, ts, HP), lambda b, i: (b, i, 0)),
            pl.BlockSpec((1, 1, MLA_WIDTH, ts), lambda b, i: (b, i, 0, 0)),
        ],
        out_shape=[
            jax.ShapeDtypeStruct((B, HP, S), BF16),
            jax.ShapeDtypeStruct((B, S, HP), BF16),
            jax.ShapeDtypeStruct((B, S // ts, MLA_WIDTH, ts), BF16),
        ],
        compiler_params=_cparams("parallel", "parallel"),
        name="mla_prep",
    )(zb, tabs, qg, wq_bf, kg, wk_bf, wvt_bf)


def _mla_attn_kernel(qt_ref, k_ref, vt_ref, o_ref, *, tk, nk):
    tq = qt_ref.shape[2]
    qts = [qt_ref[0, 128 * hh:128 * hh + 128, :] for hh in range(2)]

    def body(i, carry):
        ks = pl.multiple_of(i * tk, tk)
        new = []
        for hh in range(2):
            m, l, acc = carry[hh]
            s = _dot(k_ref[0, pl.ds(ks, tk), 128 * hh:128 * hh + 128], qts[hh])
            m_new = jnp.maximum(m, jnp.max(s, axis=0, keepdims=True))
            alpha = jnp.exp2(m - m_new)
            p = jnp.exp2(s - m_new)
            l = alpha * l + jnp.sum(p, axis=0, keepdims=True)
            acc = alpha * acc + _dot(vt_ref[0, i, 64 * hh:64 * hh + 64, :], p.astype(BF16))
            new.append((m_new, l, acc))
        return tuple(new)

    init = (jnp.full((1, tq), -jnp.inf, F32), jnp.zeros((1, tq), F32), jnp.zeros((MLA_V, tq), F32))
    (_, l0, acc0), (_, l1, acc1) = lax.fori_loop(0, nk, body, (init, init))
    o_ref[0] = jnp.concatenate([acc0 / l0, acc1 / l1], axis=0).T


def _mla_attn_call(qt, k, vt):
    B, S, _ = k.shape
    tq = min(S, MLA_TQ)
    tk = _mla_key_tile(S)
    nk = S // tk
    return pl.pallas_call(
        functools.partial(_mla_attn_kernel, tk=tk, nk=nk),
        grid=(B, MLA_HEADS // 2, S // tq),
        in_specs=[
            pl.BlockSpec((1, 256, tq), lambda b, h, i: (b, h, i)),
            pl.BlockSpec((1, S, 256), lambda b, h, i: (b, 0, h)),
            pl.BlockSpec((1, nk, 128, tk), lambda b, h, i: (b, 0, h, 0)),
        ],
        out_specs=pl.BlockSpec((1, tq, 128), lambda b, h, i: (b, i, h)),
        out_shape=jax.ShapeDtypeStruct((B, S, MLA_WIDTH), F32),
        compiler_params=_cparams("parallel", "parallel", "arbitrary"),
        name="mla_attn",
    )(qt, k, vt)


def _ret_dir(direction, zc, tab, lrv, lrh, r_ref, y_ref):
    C = RET_CHUNK
    q = _apply_rope(zc[:, 0:256], tab, 32)
    k = _apply_rope(zc[:, 256:512], tab, 32) * (HEAD_DIM ** -0.5)
    v = zc[:, 512:768]
    lgv = -jnp.exp(lrv)
    lgh = -jnp.exp(lrh)
    pos = lax.broadcasted_iota(jnp.int32, (C, 1), 0).astype(F32)
    row = lax.broadcasted_iota(jnp.int32, (C, C), 0)
    col = lax.broadcasted_iota(jnp.int32, (C, C), 1)
    if direction == 0:
        kw = k * jnp.exp(lgv * (C - 1.0 - pos))
        qw = q * jnp.exp(lgv * (pos + 1.0))
        mask = col <= row
        dist = (row - col).astype(F32)
    else:
        kw = k * jnp.exp(lgv * pos)
        qw = q * jnp.exp(lgv * (C - pos))
        mask = col > row
        dist = (col - row).astype(F32)
    dist = jnp.maximum(dist, 0.0)
    qb, kb, vb, kwb, qwb = (t.astype(BF16) for t in (q, k, v, kw, qw))
    for h in range(RET_HEADS):
        sl = slice(64 * h, 64 * h + 64)
        lg = lgh[h:h + 1, :]
        dmat = jnp.where(mask, jnp.exp(lg * dist), 0.0)
        sc = lax.dot_general(qb[:, sl], kb[:, sl], (_NT, ((), ())), preferred_element_type=F32) * dmat
        inner = _dot(sc.astype(BF16), vb[:, sl])
        r0 = r_ref[direction, h]
        cross = _dot(qwb[:, sl], r0.astype(BF16))
        y_ref[0, :, sl] = inner + cross
        kv = lax.dot_general(kwb[:, sl], vb[:, sl], (_TN, ((), ())), preferred_element_type=F32)
        r_ref[direction, h] = r0 * jnp.exp(lg[:, 0:64] * C) + kv


def _ret_kernel(zf_ref, zb_ref, tf_ref, tb_ref, lrv_ref, lrh_ref, yf_ref, yb_ref, r_ref):
    @pl.when(pl.program_id(1) == 0)
    def _():
        r_ref[...] = jnp.zeros_like(r_ref)

    _ret_dir(0, zf_ref[0], tf_ref[0, 0], lrv_ref[0:1, :], lrh_ref[0:4, :], r_ref, yf_ref)
    _ret_dir(1, zb_ref[0], tb_ref[0, 0], lrv_ref[1:2, :], lrh_ref[4:8, :], r_ref, yb_ref)


def _ret_call(zc, tabs, lr_vec, lr_heads):
    B, S, _ = zc.shape
    C = RET_CHUNK
    nc = S // C
    const = lambda shape: pl.BlockSpec(shape, lambda b, c: (0,) * len(shape))
    return pl.pallas_call(
        _ret_kernel,
        grid=(B, nc),
        in_specs=[
            pl.BlockSpec((1, C, ZC_COLS), lambda b, c: (b, c, 0)),
            pl.BlockSpec((1, C, ZC_COLS), lambda b, c: (b, nc - 1 - c, 0)),
            pl.BlockSpec((1, 1, C, 384), lambda b, c: (1, b, c, 0)),
            pl.BlockSpec((1, 1, C, 384), lambda b, c: (1, b, nc - 1 - c, 0)),
            const((2, 256)), const((8, 128)),
        ],
        out_specs=[
            pl.BlockSpec((1, C, 256), lambda b, c: (b, c, 0)),
            pl.BlockSpec((1, C, 256), lambda b, c: (b, nc - 1 - c, 0)),
        ],
        out_shape=[jax.ShapeDtypeStruct((B, S, 256), F32)] * 2,
        scratch_shapes=[pltpu.VMEM((2, RET_HEADS, HEAD_DIM, HEAD_DIM), F32)],
        compiler_params=_cparams("parallel", "arbitrary"),
        name="retention",
    )(zc, zc, tabs, tabs, lr_vec, lr_heads)


def _head_norm(y, avg_bd, eps):
    mu = _dot_hi(y, avg_bd)
    d = y - mu
    var = _dot_hi(d * d, avg_bd)
    return d * lax.rsqrt(var + eps)


def _outproj_kernel(x_ref, g2_ref, yf_ref, yb_ref, pg_ref, lng_ref, lnb_ref, om_ref, rf_ref, rb_ref,
                    gate_ref, gng_ref, avg_ref, wa_ref, wb_ref, wc_ref, o_ref):
    avg_bd = avg_ref[...]
    y = _head_norm(yf_ref[0] + yb_ref[0], avg_bd, RWKV_LN_EPS) * lng_ref[...] + lnb_ref[...]
    pg = pg_ref[0]
    o_a = (y + pg[:, 256:512]) * pg[:, 0:256]
    yr = _head_norm(rf_ref[0] + rb_ref[0], avg_bd, RET_LN_EPS) * gng_ref[...]
    gate = gate_ref[0]
    o_c = (gate * _sigmoid(gate)) * yr
    mixed = _dot(o_a.astype(BF16), wa_ref[...])
    mixed = mixed + _dot(om_ref[0].astype(BF16), wb_ref[...])
    mixed = mixed + _dot(o_c.astype(BF16), wc_ref[...])
    o_ref[0] = x_ref[0] + g2_ref[0] * mixed


def _outproj_call(x, g2, yf, yb, pg, ln_g, ln_b, o_mla, rf, rb, zc, gn_g, avg_bd, wa, wb, wc):
    B, S, D = x.shape
    tm = min(S, 512)
    tok = lambda n: pl.BlockSpec((1, tm, n), lambda b, i: (b, i, 0))
    const = lambda shape: pl.BlockSpec(shape, lambda b, i: (0,) * len(shape))
    return pl.pallas_call(
        _outproj_kernel,
        grid=(B, S // tm),
        in_specs=[
            tok(D),
            pl.BlockSpec((1, 1, D), lambda b, i: (b, 0, 0)),
            tok(256), tok(256), tok(512), const((1, 256)), const((1, 256)),
            tok(512), tok(256), tok(256),
            pl.BlockSpec((1, tm, 256), lambda b, i: (b, i, 3)),
            const((1, 256)), const((256, 256)),
            const((256, D)), const((512, D)), const((256, D)),
        ],
        out_specs=tok(D),
        out_shape=jax.ShapeDtypeStruct((B, S, D), F32),
        compiler_params=_cparams("parallel", "parallel"),
        name="outproj",
    )(x, g2, yf, yb, pg, ln_g, ln_b, o_mla, rf, rb, zc, gn_g, avg_bd, wa, wb, wc)


def _block_diag2(w):
    z = jnp.zeros_like(w[0])
    return jnp.concatenate([jnp.concatenate([w[0], z], axis=1), jnp.concatenate([z, w[1]], axis=1)], axis=0)


def _pad_inproj(w):
    D = w.shape[0]
    o = RWKV_COLS + MLA_Q_RANK + MLA_KV_RANK
    return jnp.concatenate(
        [w[:, :o], jnp.zeros((D, 64), w.dtype), w[:, o:o + MLA_ROPE], jnp.zeros((D, 32), w.dtype), w[:, o + MLA_ROPE:]],
        axis=1)


def _pad_wq(w):
    r = w.shape[0]
    w = w.reshape(r, MLA_HEADS, MLA_NOPE + MLA_ROPE)
    w = jnp.concatenate([w, jnp.zeros((r, MLA_HEADS, 32), w.dtype)], axis=2)
    return w.reshape(r, MLA_HEADS * MLA_HEAD_PAD)


def _split_wkv(w):
    r = w.shape[0]
    w = w.reshape(r, MLA_HEADS, MLA_NOPE + MLA_V)
    wk = jnp.concatenate([w[:, :, :MLA_NOPE], jnp.zeros((r, MLA_HEADS, 64), w.dtype)], axis=2)
    return wk.reshape(r, MLA_HEADS * MLA_HEAD_PAD), w[:, :, MLA_NOPE:].reshape(r, MLA_WIDTH)


def kernel(x, c, positions, w_ada, b_ada, w_ff1_in, w_ff1_out, w_ff2_in, w_ff2_out, w_in, w_out, rwkv_mu, rwkv_w0,
           rwkv_w_up, rwkv_a0, rwkv_a_up, rwkv_g_up, rwkv_k_k, rwkv_k_a, rwkv_r_k, rwkv_ln_g, rwkv_ln_b,
           mla_q_norm_g, mla_w_uq, mla_kv_norm_g, mla_w_ukv, ret_log_rate, ret_gn_g, final_norm_g):
    B, S, D = x.shape
    L = w_ada.shape[0]
    mod = _mod_call(c, w_ada, b_ada)
    tabs = _rope_call(positions)

    head_id = jnp.arange(256) // HEAD_DIM
    same_head = (head_id[:, None] == head_id[None, :]).astype(F32)
    ones_bd = same_head
    avg_bd = same_head / HEAD_DIM
    final_g = final_norm_g.reshape(1, D)

    for l in range(L):
        m = [mod[l, :, i * D:(i + 1) * D].reshape(B, 1, D) for i in range(N_MOD)]
        sh1, sc1, g1, sh2, sc2, g2, sh3, sc3, g3 = m

        x = _ffn_call(x, sh1, sc1, g1, w_ff1_in[l].astype(BF16), w_ff1_out[l].astype(BF16), final_g, False)

        za, zb, zc = _inproj_call(x, sh2, sc2, _pad_inproj(w_in[l]).astype(BF16))

        pc, pd, pg = _rwkv_prep_call(
            za, rwkv_mu[l].reshape(1, -1), rwkv_w0[l].reshape(1, 512), _block_diag2(rwkv_w_up[l]),
            rwkv_a0[l].reshape(1, 512), _block_diag2(rwkv_a_up[l]), rwkv_g_up[l],
            rwkv_k_k[l].reshape(1, 256), rwkv_k_a[l].reshape(1, 256), rwkv_r_k[l].reshape(1, 256), ones_bd)
        yf, yb = _rwkv_scan_call(pc, pd)

        wk, wv = _split_wkv(mla_w_ukv[l])
        qt, k, vt = _mla_prep_call(zb, tabs, mla_q_norm_g[l].reshape(1, -1), _pad_wq(mla_w_uq[l]).astype(BF16),
                                   mla_kv_norm_g[l].reshape(1, -1), wk.astype(BF16), wv.T.astype(BF16))
        o_mla = _mla_attn_call(qt, k, vt)

        lr = ret_log_rate[l]
        lr_vec = jnp.repeat(lr, HEAD_DIM, axis=1)
        lr_heads = jnp.broadcast_to(lr.reshape(8, 1), (8, 128))
        rf, rb = _ret_call(zc, tabs, lr_vec, lr_heads)

        wo = w_out[l].astype(BF16)
        x = _outproj_call(x, g2, yf, yb, pg, rwkv_ln_g[l].reshape(1, 256), rwkv_ln_b[l].reshape(1, 256), o_mla,
                          rf, rb, zc, ret_gn_g[l].reshape(1, 256), avg_bd,
                          wo[0:256], wo[256:768], wo[768:1024])

        x = _ffn_call(x, sh3, sc3, g3, w_ff2_in[l].astype(BF16), w_ff2_out[l].astype(BF16), final_g, l == L - 1)
    return x
```

```python
import functools
import math

import jax
import jax.numpy as jnp
from jax import lax
from jax.experimental import pallas as pl
from jax.experimental.pallas import tpu as pltpu

F32 = jnp.float32
BF16 = jnp.bfloat16
HIGHEST = lax.Precision.HIGHEST

D_MODEL = 1024
HEAD_DIM = 64
RWKV_WIDTH = 256
RWKV_HEADS = 4
RWKV_LORA = 64
RWKV_GATE_LORA = 128
RWKV_COLS = 1152
MLA_HEADS = 8
MLA_NOPE = 64
MLA_ROPE = 32
MLA_V = 64
MLA_Q_RANK = 384
MLA_KV_RANK = 256
MLA_WIDTH = 512
MLA_HEAD_PAD = 128
MLA_TQ = 1024
MLA_QCOLS = 512
MLA_TK = 512
MLA_ONES_ROWS = 16
RET_WIDTH = 256
RET_HEADS = 4
RET_CHUNK = 128
RET_BATCH_PER_STEP = 4
RWKV_CHUNK = 64
RWKV_BATCH_PER_STEP = 4
RWKV_PASSES = 1
D_FF = 2816
N_MOD = 9
ROPE_BASE = 10000.0
NORM_EPS = 1e-6
RWKV_LN_EPS = 64e-5
RET_LN_EPS = 1e-5

VMEM_LIMIT = 48 * 1024 * 1024


def _cparams(*sem):
    return pltpu.CompilerParams(dimension_semantics=sem, vmem_limit_bytes=VMEM_LIMIT)


def _sigmoid(x):
    return 1.0 / (1.0 + jnp.exp(-x))


def _dot(a, b):
    return jnp.dot(a, b, preferred_element_type=F32)


def _dot_hi(a, b):
    return jnp.dot(a, b, precision=HIGHEST, preferred_element_type=F32)


def _split(a):
    hi = a.astype(BF16)
    lo = (a - hi.astype(F32)).astype(BF16)
    return hi, lo


def _dg3(a, b, dims):
    ah, al = _split(a)
    bh, bl = _split(b)
    dn = (dims, ((), ()))
    out = lax.dot_general(ah, bh, dn, preferred_element_type=F32)
    out = out + lax.dot_general(ah, bl, dn, preferred_element_type=F32)
    out = out + lax.dot_general(al, bh, dn, preferred_element_type=F32)
    return out


_NN = ((1,), (0,))
_NT = ((1,), (1,))
_TN = ((0,), (0,))


def _rms_mod(x, sh, sc):
    ms = jnp.mean(x * x, axis=-1, keepdims=True)
    return (x * lax.rsqrt(ms + NORM_EPS)) * (1.0 + sc) + sh


def _mod_kernel(c_ref, w_ref, b_ref, o_ref):
    c = c_ref[...]
    cond = c * _sigmoid(c)
    o_ref[0] = _dot_hi(cond, w_ref[0]) + b_ref[0]


def _mod_call(c, w_ada, b_ada):
    L, D, N = w_ada.shape
    B = c.shape[0]
    tn = 1152
    return pl.pallas_call(
        _mod_kernel,
        grid=(L, N // tn),
        in_specs=[
            pl.BlockSpec((B, D), lambda l, j: (0, 0)),
            pl.BlockSpec((1, D, tn), lambda l, j: (l, 0, j)),
            pl.BlockSpec((1, 1, tn), lambda l, j: (l, 0, j)),
        ],
        out_specs=pl.BlockSpec((1, B, tn), lambda l, j: (l, 0, j)),
        out_shape=jax.ShapeDtypeStruct((L, B, N), F32),
        compiler_params=_cparams("parallel", "parallel"),
        name="adaln_mod",
    )(c, w_ada, b_ada.reshape(L, 1, N))


def _rope_kernel(pos_ref, inv_ref, mc_ref, m1_ref, ma_ref, mb_ref, o_ref):
    pos = pos_ref[0].astype(F32)
    for i in range(2):
        ang = pos * inv_ref[i:i + 1, :]
        cos = jnp.cos(ang)
        sin = jnp.sin(ang)
        o_ref[i, 0, :, 0:128] = cos * mc_ref[i:i + 1, :] + m1_ref[i:i + 1, :]
        o_ref[i, 0, :, 128:256] = sin * ma_ref[i:i + 1, :]
        o_ref[i, 0, :, 256:384] = sin * mb_ref[i:i + 1, :]


def _rope_consts():
    lane = jnp.arange(128)
    inv_m = ROPE_BASE ** (-jnp.arange(0, MLA_ROPE, 2, dtype=F32) / MLA_ROPE)
    in_rope = (lane >= 64) & (lane < 96)
    inv0 = jnp.where(in_rope, inv_m[(lane - 64) % 16], 0.0)
    mc0 = in_rope.astype(F32)
    m10 = (lane < 64).astype(F32)
    ma0 = jnp.where((lane >= 64) & (lane < 80), -1.0, 0.0)
    mb0 = jnp.where((lane >= 80) & (lane < 96), 1.0, 0.0)
    inv_r = ROPE_BASE ** (-jnp.arange(0, HEAD_DIM, 2, dtype=F32) / HEAD_DIM)
    inv1 = inv_r[lane % 32]
    mc1 = jnp.ones((128,), F32)
    m11 = jnp.zeros((128,), F32)
    ma1 = jnp.where((lane % 64) < 32, -1.0, 0.0)
    mb1 = jnp.where((lane % 64) >= 32, 1.0, 0.0)
    st = lambda a, b: jnp.stack([a, b]).astype(F32)
    return st(inv0, inv1), st(mc0, mc1), st(m10, m11), st(ma0, ma1), st(mb0, mb1)


def _rope_call(positions):
    B, S = positions.shape
    ts = min(S, 512)
    consts = _rope_consts()
    cspec = pl.BlockSpec((2, 128), lambda b, i: (0, 0))
    return pl.pallas_call(
        _rope_kernel,
        grid=(B, S // ts),
        in_specs=[pl.BlockSpec((1, ts, 1), lambda b, i: (b, i, 0))] + [cspec] * 5,
        out_specs=pl.BlockSpec((2, 1, ts, 384), lambda b, i: (0, b, i, 0)),
        out_shape=jax.ShapeDtypeStruct((2, B, S, 384), F32),
        compiler_params=_cparams("parallel", "parallel"),
        name="rope_tables",
    )(positions.reshape(B, S, 1), *consts)


def _apply_rope(x, tab, shift):
    n = x.shape[1] // 128
    width = x.shape[1]
    cc = jnp.tile(tab[:, 0:128], (1, n))
    sa = jnp.tile(tab[:, 128:256], (1, n))
    sb = jnp.tile(tab[:, 256:384], (1, n))
    return x * cc + pltpu.roll(x, width - shift, axis=1) * sa + pltpu.roll(x, shift, axis=1) * sb


def _ffn_kernel(x_ref, sh_ref, sc_ref, g_ref, wg_ref, wu_ref, wo_ref, fg_ref, o_ref, h_scr, acc_scr, *, nj, final):
    j = pl.program_id(2)

    @pl.when(j == 0)
    def _():
        h = _rms_mod(x_ref[0], sh_ref[0], sc_ref[0])
        h_scr[...] = h.astype(BF16)
        acc_scr[...] = jnp.zeros_like(acc_scr)

    h = h_scr[...]
    gate = _dot(h, wg_ref[...])
    up = _dot(h, wu_ref[...])
    act = (gate * _sigmoid(gate) * up).astype(BF16)
    acc_scr[...] += _dot(act, wo_ref[...])

    @pl.when(j == nj - 1)
    def _():
        y = x_ref[0] + (0.5 * g_ref[0]) * acc_scr[...]
        if final:
            ms = jnp.mean(y * y, axis=-1, keepdims=True)
            y = (y * lax.rsqrt(ms + NORM_EPS)) * fg_ref[...]
        o_ref[0] = y


def _ffn_call(x, sh, sc, g, w_in_bf, w_out_bf, final_g, final):
    B, S, D = x.shape
    tm = min(S, 512)
    tf = 1408
    nj = D_FF // tf
    vec = pl.BlockSpec((1, 1, D), lambda b, i, j: (b, 0, 0))
    return pl.pallas_call(
        functools.partial(_ffn_kernel, nj=nj, final=final),
        grid=(B, S // tm, nj),
        in_specs=[
            pl.BlockSpec((1, tm, D), lambda b, i, j: (b, i, 0)),
            vec, vec, vec,
            pl.BlockSpec((D, tf), lambda b, i, j: (0, j)),
            pl.BlockSpec((D, tf), lambda b, i, j: (0, j + nj)),
            pl.BlockSpec((tf, D), lambda b, i, j: (j, 0)),
            pl.BlockSpec((1, D), lambda b, i, j: (0, 0)),
        ],
        out_specs=pl.BlockSpec((1, tm, D), lambda b, i, j: (b, i, 0)),
        out_shape=jax.ShapeDtypeStruct((B, S, D), F32),
        scratch_shapes=[pltpu.VMEM((tm, D), BF16), pltpu.VMEM((tm, D), F32)],
        compiler_params=_cparams("parallel", "parallel", "arbitrary"),
        name="ffn",
    )(x, sh, sc, g, w_in_bf, w_in_bf, w_out_bf, final_g)


ZB_COLS = MLA_Q_RANK + MLA_KV_RANK + MLA_HEAD_PAD
ZC_COLS = 4 * RET_WIDTH
Z_COLS = RWKV_COLS + ZB_COLS + ZC_COLS


def _inproj_kernel(x_ref, sh_ref, sc_ref, w_ref, tab_ref, za_ref, zb_ref, zc_ref):
    h = _rms_mod(x_ref[0], sh_ref[0], sc_ref[0]).astype(BF16)
    z = _dot(h, w_ref[...])
    za_ref[0] = z[:, 0:RWKV_COLS]
    zb_ref[0] = z[:, RWKV_COLS:RWKV_COLS + ZB_COLS]
    o = RWKV_COLS + ZB_COLS
    tab = tab_ref[0, 0]
    zc_ref[0, :, 0:256] = _apply_rope(z[:, o:o + 256], tab, 32)
    zc_ref[0, :, 256:512] = _apply_rope(z[:, o + 256:o + 512], tab, 32) * (HEAD_DIM ** -0.5)
    zc_ref[0, :, 512:1024] = z[:, o + 512:Z_COLS]


def _inproj_call(x, sh, sc, wz_bf, tabs):
    B, S, D = x.shape
    tm = min(S, 512)
    vec = pl.BlockSpec((1, 1, D), lambda b, i: (b, 0, 0))
    out = lambda n: pl.BlockSpec((1, tm, n), lambda b, i: (b, i, 0))
    return pl.pallas_call(
        _inproj_kernel,
        grid=(B, S // tm),
        in_specs=[
            pl.BlockSpec((1, tm, D), lambda b, i: (b, i, 0)),
            vec, vec,
            pl.BlockSpec((D, Z_COLS), lambda b, i: (0, 0)),
            pl.BlockSpec((1, 1, tm, 384), lambda b, i: (1, b, i, 0)),
        ],
        out_specs=[out(RWKV_COLS), out(ZB_COLS), out(ZC_COLS)],
        out_shape=[jax.ShapeDtypeStruct((B, S, n), F32) for n in (RWKV_COLS, ZB_COLS, ZC_COLS)],
        compiler_params=_cparams("parallel", "parallel"),
        name="inproj",
    )(x, sh, sc, wz_bf, tabs)


def _rwkv_prep_kernel(z_ref, zp_ref, zn_ref, mu_ref, w0_ref, wup_ref, a0_ref, aup_ref, gup_ref,
                      kk_ref, ka_ref, rk_ref, ones_ref, pc_ref, pd_ref, pg_ref, *, ts, nt):
    i = pl.program_id(1)
    z = z_ref[0]
    row = lax.broadcasted_iota(jnp.int32, (ts, 1), 0)
    prev_edge = jnp.where(i > 0, zp_ref[0, 7:8, :], 0.0)
    next_edge = jnp.where(i < nt - 1, zn_ref[0, 0:1, :], 0.0)
    prev = jnp.where(row == 0, prev_edge, pltpu.roll(z, 1, axis=0))
    nxt = jnp.where(row == ts - 1, next_edge, pltpu.roll(z, ts - 1, axis=0))
    zs = z + mu_ref[...] * (0.5 * (prev + nxt) - z)

    r = zs[:, 0:256]
    k = zs[:, 256:512]
    v = zs[:, 512:768]
    g_lo = zs[:, 768:896]
    w_lo = zs[:, 896:1024]
    a_lo = zs[:, 1024:1152]

    w_raw = w0_ref[...] + _dg3(jnp.tanh(w_lo), wup_ref[...], _NN)
    logw = (-math.exp(-0.5)) * _sigmoid(w_raw)
    a = _sigmoid(a0_ref[...] + _dg3(a_lo, aup_ref[...], _NN))
    g = _dg3(_sigmoid(g_lo), gup_ref[...], _NN)

    ones_bd = ones_ref[...]
    kk0 = k * kk_ref[...]
    nrm = jnp.sqrt(_dot_hi(kk0 * kk0, ones_bd))
    kk = kk0 / jnp.maximum(nrm, 1e-12)
    bonus = _dot_hi(r * k * rk_ref[...], ones_bd) * v

    pc_ref[0, :, 0:256] = r
    pc_ref[0, :, 256:512] = v
    pc_ref[0, :, 512:768] = kk
    for d in range(2):
        a_d = a[:, 256 * d:256 * d + 256]
        pd_ref[d, 0, :, 0:256] = k * (1.0 + (a_d - 1.0) * ka_ref[...])
        pd_ref[d, 0, :, 256:512] = kk * a_d
        pd_ref[d, 0, :, 512:768] = logw[:, 256 * d:256 * d + 256]
    pg_ref[0, :, 0:256] = g
    pg_ref[0, :, 256:512] = bonus


def _rwkv_prep_call(za, mu, w0, wup_bd, a0, aup_bd, gup, k_k, k_a, r_k, ones_bd):
    B, S, _ = za.shape
    ts = min(S, 512)
    nt = S // ts
    nb8 = S // 8
    const = lambda shape: pl.BlockSpec(shape, lambda b, i: (0,) * len(shape))
    return pl.pallas_call(
        functools.partial(_rwkv_prep_kernel, ts=ts, nt=nt),
        grid=(B, nt),
        in_specs=[
            pl.BlockSpec((1, ts, RWKV_COLS), lambda b, i: (b, i, 0)),
            pl.BlockSpec((1, 8, RWKV_COLS), lambda b, i: (b, jnp.maximum(i * (ts // 8) - 1, 0), 0)),
            pl.BlockSpec((1, 8, RWKV_COLS), lambda b, i: (b, jnp.minimum((i + 1) * (ts // 8), nb8 - 1), 0)),
            const((1, RWKV_COLS)),
            const((1, 512)), const((128, 512)), const((1, 512)), const((128, 512)), const((128, 256)),
            const((1, 256)), const((1, 256)), const((1, 256)), const((256, 256)),
        ],
        out_specs=[
            pl.BlockSpec((1, ts, 768), lambda b, i: (b, i, 0)),
            pl.BlockSpec((2, 1, ts, 768), lambda b, i: (0, b, i, 0)),
            pl.BlockSpec((1, ts, 512), lambda b, i: (b, i, 0)),
        ],
        out_shape=[
            jax.ShapeDtypeStruct((B, S, 768), F32),
            jax.ShapeDtypeStruct((2, B, S, 768), F32),
            jax.ShapeDtypeStruct((B, S, 512), F32),
        ],
        compiler_params=_cparams("parallel", "parallel"),
        name="rwkv_prep",
    )(za, za, za, mu, w0, wup_bd, a0, aup_bd, gup, k_k, k_a, r_k, ones_bd)


def _rwkv_dir_operands(direction, pc, pd):
    C = RWKV_CHUNK
    r = pc[:, 0:256]
    v = pc[:, 256:512]
    kk = pc[:, 512:768]
    kmod = pd[:, 0:256]
    kka = pd[:, 256:512]
    logw = pd[:, 512:768]

    row = lax.broadcasted_iota(jnp.int32, (C, C), 0)
    col = lax.broadcasted_iota(jnp.int32, (C, C), 1)
    tri = ((col <= row) if direction == 0 else (col >= row)).astype(F32)
    l_in = _dot_hi(tri, logw)
    l_ex = l_in - logw
    l_tot = l_in[C - 1:C, :] if direction == 0 else l_in[0:1, :]
    e_ex = jnp.exp(l_ex)
    e_neg = jnp.exp(-l_in)
    e_tot = jnp.exp(l_tot - l_in)
    kkg = kk * e_ex
    bbar = kka * e_neg
    kbar = kmod * e_neg
    bg = kka * e_tot
    kg = kmod * e_tot
    rg = r * jnp.exp(l_in) if direction == 0 else r * e_ex
    g_tot = jnp.exp(l_tot)
    return dict(kkg=kkg, bbar=bbar, kbar=kbar, bg=bg, kg=kg, rg=rg, v=v, g_tot=g_tot)


def _operand(a, passes):
    hi = a.astype(BF16)
    lo = (a - hi.astype(F32)).astype(BF16) if passes > 1 else None
    return hi, lo


def _mm(a, b, dims):
    dn = (dims, ((), ()))
    out = lax.dot_general(a[0], b[0], dn, preferred_element_type=F32)
    if b[1] is not None:
        out = out + lax.dot_general(a[0], b[1], dn, preferred_element_type=F32)
    if a[1] is not None:
        out = out + lax.dot_general(a[1], b[0], dn, preferred_element_type=F32)
    return out


def _rwkv_scan_kernel(pcf_ref, pcb_ref, pdf_ref, pdb_ref, yf_ref, yb_ref, s_ref):
    C = RWKV_CHUNK
    P = RWKV_PASSES

    @pl.when(pl.program_id(1) == 0)
    def _():
        s_ref[...] = jnp.zeros_like(s_ref)

    row = lax.broadcasted_iota(jnp.int32, (C, C), 0)
    col = lax.broadcasted_iota(jnp.int32, (C, C), 1)
    eye = col == row
    ident = jnp.where(eye, 1.0, 0.0)
    nb = pcf_ref.shape[0]
    ops = {}
    for bb in range(nb):
        ops[bb, 0] = _rwkv_dir_operands(0, pcf_ref[bb], pdf_ref[0, bb])
        ops[bb, 1] = _rwkv_dir_operands(1, pcb_ref[bb], pdb_ref[0, bb])
    earlier = (col < row, col > row)
    ymask = (col <= row, col > row)
    units = [(bb, d, h) for bb in range(nb) for d in range(2) for h in range(RWKV_HEADS)]
    nu = len(units)
    sl = lambda h: slice(64 * h, 64 * h + 64)
    s0 = [s_ref[u] for u in units]

    def head(name):
        return [_operand(ops[bb, d][name][:, sl(h)], P) for bb, d, h in units]

    kkg, bbar, kbar, bg, kg, rg, v = (head(n) for n in ("kkg", "bbar", "kbar", "bg", "kg", "rg", "v"))
    akk = [jnp.where(earlier[d], _mm(kkg[i], bbar[i], _NT), 0.0) for i, (_, d, _) in enumerate(units)]
    bk = [jnp.where(earlier[d], _mm(kkg[i], kbar[i], _NT), 0.0) for i, (_, d, _) in enumerate(units)]
    ark = [jnp.where(ymask[d], _mm(rg[i], bbar[i], _NT), 0.0) for i, (_, d, _) in enumerate(units)]
    brk = [jnp.where(ymask[d], _mm(rg[i], kbar[i], _NT), 0.0) for i, (_, d, _) in enumerate(units)]
    pw = [-a for a in akk]
    t_inv = [ident + x for x in pw]
    for _ in range(int(math.log2(C)) - 1):
        pw_o = [_operand(x, P) for x in pw]
        pw = [_mm(x, x, _NN) for x in pw_o]
        pw_o = [_operand(x, P) for x in pw]
        t_inv = [t + _mm(_operand(t, P), x, _NN) for t, x in zip(t_inv, pw_o)]
    t_o = [_operand(t, P) for t in t_inv]
    bkv = [_mm(_operand(b, P), x, _NN) for b, x in zip(bk, v)]
    w_m = [_mm(t, x, _NN) for t, x in zip(t_o, kkg)]
    u_m = [_mm(t, _operand(x, P), _NN) for t, x in zip(t_o, bkv)]
    w_o = [_operand(x, P) for x in w_m]
    u_o = [_operand(x, P) for x in u_m]
    ark_o = [_operand(x, P) for x in ark]
    brk_o = [_operand(x, P) for x in brk]
    m_m = [jnp.where(eye, ops[bb, d]["g_tot"][:, sl(h)], 0.0) - _mm(w_o[i], bg[i], _TN)
           for i, (bb, d, h) in enumerate(units)]
    n_m = [_mm(v[i], kg[i], _TN) - _mm(u_o[i], bg[i], _TN) for i in range(nu)]
    q_m = [ops[bb, d]["rg"][:, sl(h)] - _mm(ark_o[i], w_o[i], _NN) for i, (bb, d, h) in enumerate(units)]
    y_i = [_mm(brk_o[i], v[i], _NN) - _mm(ark_o[i], u_o[i], _NN) for i in range(nu)]
    s_o = [_operand(s, P) for s in s0]
    y = [y_i[i] + _mm(_operand(q_m[i], P), s_o[i], _NT) for i in range(nu)]
    s_new = [_mm(s_o[i], _operand(m_m[i], P), _NN) + n_m[i] for i in range(nu)]
    for i, (bb, d, h) in enumerate(units):
        (yf_ref if d == 0 else yb_ref)[bb, :, sl(h)] = y[i]
        s_ref[bb, d, h] = s_new[i]


def _rwkv_scan_call(pc, pd):
    B, S, _ = pc.shape
    C = RWKV_CHUNK
    nc = S // C
    nb = math.gcd(B, RWKV_BATCH_PER_STEP)
    return pl.pallas_call(
        _rwkv_scan_kernel,
        grid=(B // nb, nc),
        in_specs=[
            pl.BlockSpec((nb, C, 768), lambda b, c: (b, c, 0)),
            pl.BlockSpec((nb, C, 768), lambda b, c: (b, nc - 1 - c, 0)),
            pl.BlockSpec((1, nb, C, 768), lambda b, c: (0, b, c, 0)),
            pl.BlockSpec((1, nb, C, 768), lambda b, c: (1, b, nc - 1 - c, 0)),
        ],
        out_specs=[
            pl.BlockSpec((nb, C, 256), lambda b, c: (b, c, 0)),
            pl.BlockSpec((nb, C, 256), lambda b, c: (b, nc - 1 - c, 0)),
        ],
        out_shape=[jax.ShapeDtypeStruct((B, S, 256), F32)] * 2,
        scratch_shapes=[pltpu.VMEM((nb, 2, RWKV_HEADS, HEAD_DIM, HEAD_DIM), F32)],
        compiler_params=_cparams("parallel", "arbitrary"),
        name="rwkv_scan",
    )(pc, pc, pd, pd)


def _mla_prep_kernel(zb_ref, tab_ref, qg_ref, wq_ref, kg_ref, wk_ref, wvt_ref, qt_ref, k_ref, vt_ref):
    zb = zb_ref[0]
    tab = tab_ref[0, 0]
    cq = zb[:, 0:MLA_Q_RANK]
    ckv = zb[:, MLA_Q_RANK:MLA_Q_RANK + MLA_KV_RANK]
    kpe = zb[:, MLA_Q_RANK + MLA_KV_RANK:ZB_COLS]

    def rms(t):
        return t * lax.rsqrt(jnp.mean(t * t, axis=-1, keepdims=True) + NORM_EPS)

    cqn = (rms(cq) * qg_ref[...]).astype(BF16)
    ckvn = (rms(ckv) * kg_ref[...]).astype(BF16)
    q = _dot(cqn, wq_ref[...])
    q = _apply_rope(q, tab, 16) * (math.log2(math.e) * (MLA_NOPE + MLA_ROPE) ** -0.5)
    k = _dot(ckvn, wk_ref[...])
    kpe = _apply_rope(kpe, tab, 16)
    k = k + jnp.tile(kpe, (1, MLA_HEADS))
    qt_ref[0] = q.T.astype(BF16)
    k_ref[0] = k.astype(BF16)
    vt_ref[0, 0] = lax.dot_general(wvt_ref[...], ckvn, (_NT, ((), ())), preferred_element_type=F32).astype(BF16)


def _mla_key_tile(S):
    return min(S, MLA_TK)


def _mla_prep_call(zb, tabs, qg, wq_bf, kg, wk_bf, wvt_bf):
    B, S, _ = zb.shape
    ts = _mla_key_tile(S)
    HP = MLA_HEADS * MLA_HEAD_PAD
    const = lambda shape: pl.BlockSpec(shape, lambda b, i: (0,) * len(shape))
    return pl.pallas_call(
        _mla_prep_kernel,
        grid=(B, S // ts),
        in_specs=[
            pl.BlockSpec((1, ts, ZB_COLS), lambda b, i: (b, i, 0)),
            pl.BlockSpec((1, 1, ts, 384), lambda b, i: (0, b, i, 0)),
            const((1, MLA_Q_RANK)), const((MLA_Q_RANK, HP)),
            const((1, MLA_KV_RANK)), const((MLA_KV_RANK, HP)), const((MLA_WIDTH, MLA_KV_RANK)),
        ],
        out_specs=[
            pl.BlockSpec((1, HP, ts), lambda b, i: (b, 0, i)),
            pl.BlockSpec((1, ts, HP), lambda b, i: (b, i, 0)),
            pl.BlockSpec((1, 1, MLA_WIDTH, ts), lambda b, i: (b, i, 0, 0)),
        ],
        out_shape=[
            jax.ShapeDtypeStruct((B, HP, S), BF16),
            jax.ShapeDtypeStruct((B, S, HP), BF16),
            jax.ShapeDtypeStruct((B, S // ts, MLA_WIDTH, ts), BF16),
        ],
        compiler_params=_cparams("parallel", "parallel"),
        name="mla_prep",
    )(zb, tabs, qg, wq_bf, kg, wk_bf, wvt_bf)


def _mla_attn_kernel(qt_ref, k_ref, vt_ref, o_ref, *, tk, nk):
    tq = qt_ref.shape[2]
    cw = min(tq, MLA_QCOLS)
    ncol = tq // cw
    chunks = [(hh, c) for hh in range(2) for c in range(ncol)]
    ones = jnp.ones((MLA_ONES_ROWS, tk), BF16)

    def body(i, carry):
        ks = pl.multiple_of(i * tk, tk)
        scores = [_dot(k_ref[0, pl.ds(ks, tk), 128 * hh:128 * hh + 128],
                       qt_ref[0, 128 * hh:128 * hh + 128, c * cw:(c + 1) * cw]) for hh, c in chunks]
        stats = []
        for j, s in enumerate(scores):
            m = carry[j][0]
            m_new = jnp.maximum(m, jnp.max(s, axis=0, keepdims=True))
            stats.append((m_new, jnp.exp2(m - m_new), jnp.exp2(s - m_new).astype(BF16)))
        new = []
        for j, (hh, c) in enumerate(chunks):
            m_new, alpha, p = stats[j]
            vt_ones = jnp.concatenate([vt_ref[0, i, 64 * hh:64 * hh + 64, :], ones], axis=0)
            new.append((m_new, alpha * carry[j][1] + _dot(vt_ones, p)))
        return tuple(new)

    init = (jnp.full((1, cw), -jnp.inf, F32), jnp.zeros((MLA_V + MLA_ONES_ROWS, cw), F32))
    res = lax.fori_loop(0, nk, body, (init,) * len(chunks))
    out_t = [jnp.concatenate([res[hh * ncol + c][1][0:MLA_V] / res[hh * ncol + c][1][MLA_V:MLA_V + 1]
                              for c in range(ncol)], axis=1) for hh in range(2)]
    o_ref[0] = jnp.concatenate(out_t, axis=0).T


def _mla_attn_call(qt, k, vt):
    B, S, _ = k.shape
    tq = min(S, MLA_TQ)
    tk = _mla_key_tile(S)
    nk = S // tk
    return pl.pallas_call(
        functools.partial(_mla_attn_kernel, tk=tk, nk=nk),
        grid=(B, MLA_HEADS // 2, S // tq),
        in_specs=[
            pl.BlockSpec((1, 256, tq), lambda b, h, i: (b, h, i)),
            pl.BlockSpec((1, S, 256), lambda b, h, i: (b, 0, h)),
            pl.BlockSpec((1, nk, 128, tk), lambda b, h, i: (b, 0, h, 0)),
        ],
        out_specs=pl.BlockSpec((1, tq, 128), lambda b, h, i: (b, i, h)),
        out_shape=jax.ShapeDtypeStruct((B, S, MLA_WIDTH), F32),
        compiler_params=_cparams("parallel", "parallel", "arbitrary"),
        name="mla_attn",
    )(qt, k, vt)


def _ret_dir_operands(direction, zc, lrv, lrh):
    C = RET_CHUNK
    q = zc[:, 0:256]
    k = zc[:, 256:512]
    v = zc[:, 512:768]
    lgv = -jnp.exp(lrv)
    lgh = -jnp.exp(lrh)
    pos = lax.broadcasted_iota(jnp.int32, (C, 1), 0).astype(F32)
    row = lax.broadcasted_iota(jnp.int32, (C, C), 0)
    col = lax.broadcasted_iota(jnp.int32, (C, C), 1)
    if direction == 0:
        kw = k * jnp.exp(lgv * (C - 1.0 - pos))
        qw = q * jnp.exp(lgv * (pos + 1.0))
        mask = col <= row
        dist = (row - col).astype(F32)
    else:
        kw = k * jnp.exp(lgv * pos)
        qw = q * jnp.exp(lgv * (C - pos))
        mask = col > row
        dist = (col - row).astype(F32)
    dist = jnp.maximum(dist, 0.0)
    dmats = [jnp.where(mask, jnp.exp(lgh[h:h + 1, :] * dist), 0.0) for h in range(RET_HEADS)]
    decays = [jnp.exp(lgh[h:h + 1, 0:64] * C) for h in range(RET_HEADS)]
    qb, kb, vb, kwb, qwb = (t.astype(BF16) for t in (q, k, v, kw, qw))
    return dict(q=qb, k=kb, v=vb, kw=kwb, qw=qwb, dmat=dmats, decay=decays)


def _ret_kernel(zf_ref, zb_ref, lrv_ref, lrh_ref, yf_ref, yb_ref, r_ref):
    @pl.when(pl.program_id(1) == 0)
    def _():
        r_ref[...] = jnp.zeros_like(r_ref)

    nb = zf_ref.shape[0]
    ops = {}
    for bb in range(nb):
        ops[bb, 0] = _ret_dir_operands(0, zf_ref[bb], lrv_ref[0:1, :], lrh_ref[0:4, :])
        ops[bb, 1] = _ret_dir_operands(1, zb_ref[bb], lrv_ref[1:2, :], lrh_ref[4:8, :])
    units = [(bb, d, h) for bb in range(nb) for d in range(2) for h in range(RET_HEADS)]
    sl = lambda h: slice(64 * h, 64 * h + 64)
    r0 = [r_ref[u] for u in units]
    sc = [lax.dot_general(ops[bb, d]["q"][:, sl(h)], ops[bb, d]["k"][:, sl(h)], (_NT, ((), ())),
                          preferred_element_type=F32) * ops[bb, d]["dmat"][h] for bb, d, h in units]
    kv = [lax.dot_general(ops[bb, d]["kw"][:, sl(h)], ops[bb, d]["v"][:, sl(h)], (_TN, ((), ())),
                          preferred_element_type=F32) for bb, d, h in units]
    cross = [_dot(ops[bb, d]["qw"][:, sl(h)], r0[i].astype(BF16)) for i, (bb, d, h) in enumerate(units)]
    inner = [_dot(sc[i].astype(BF16), ops[bb, d]["v"][:, sl(h)]) for i, (bb, d, h) in enumerate(units)]
    for i, (bb, d, h) in enumerate(units):
        (yf_ref if d == 0 else yb_ref)[bb, :, sl(h)] = inner[i] + cross[i]
        r_ref[bb, d, h] = r0[i] * ops[bb, d]["decay"][h] + kv[i]


def _ret_call(zc, lr_vec, lr_heads):
    B, S, _ = zc.shape
    C = RET_CHUNK
    nc = S // C
    nb = math.gcd(B, RET_BATCH_PER_STEP)
    const = lambda shape: pl.BlockSpec(shape, lambda b, c: (0,) * len(shape))
    return pl.pallas_call(
        _ret_kernel,
        grid=(B // nb, nc),
        in_specs=[
            pl.BlockSpec((nb, C, ZC_COLS), lambda b, c: (b, c, 0)),
            pl.BlockSpec((nb, C, ZC_COLS), lambda b, c: (b, nc - 1 - c, 0)),
            const((2, 256)), const((8, 128)),
        ],
        out_specs=[
            pl.BlockSpec((nb, C, 256), lambda b, c: (b, c, 0)),
            pl.BlockSpec((nb, C, 256), lambda b, c: (b, nc - 1 - c, 0)),
        ],
        out_shape=[jax.ShapeDtypeStruct((B, S, 256), F32)] * 2,
        scratch_shapes=[pltpu.VMEM((nb, 2, RET_HEADS, HEAD_DIM, HEAD_DIM), F32)],
        compiler_params=_cparams("parallel", "arbitrary"),
        name="retention",
    )(zc, zc, lr_vec, lr_heads)


def _head_norm(y, avg_bd, eps):
    mu = _dot_hi(y, avg_bd)
    d = y - mu
    var = _dot_hi(d * d, avg_bd)
    return d * lax.rsqrt(var + eps)


def _outproj_kernel(x_ref, g2_ref, yf_ref, yb_ref, pg_ref, lng_ref, lnb_ref, om_ref, rf_ref, rb_ref,
                    gate_ref, gng_ref, avg_ref, wa_ref, wb_ref, wc_ref, o_ref):
    avg_bd = avg_ref[...]
    y = _head_norm(yf_ref[0] + yb_ref[0], avg_bd, RWKV_LN_EPS) * lng_ref[...] + lnb_ref[...]
    pg = pg_ref[0]
    o_a = (y + pg[:, 256:512]) * pg[:, 0:256]
    yr = _head_norm(rf_ref[0] + rb_ref[0], avg_bd, RET_LN_EPS) * gng_ref[...]
    gate = gate_ref[0]
    o_c = (gate * _sigmoid(gate)) * yr
    mixed = _dot(o_a.astype(BF16), wa_ref[...])
    mixed = mixed + _dot(om_ref[0].astype(BF16), wb_ref[...])
    mixed = mixed + _dot(o_c.astype(BF16), wc_ref[...])
    o_ref[0] = x_ref[0] + g2_ref[0] * mixed


def _outproj_call(x, g2, yf, yb, pg, ln_g, ln_b, o_mla, rf, rb, zc, gn_g, avg_bd, wa, wb, wc):
    B, S, D = x.shape
    tm = min(S, 512)
    tok = lambda n: pl.BlockSpec((1, tm, n), lambda b, i: (b, i, 0))
    const = lambda shape: pl.BlockSpec(shape, lambda b, i: (0,) * len(shape))
    return pl.pallas_call(
        _outproj_kernel,
        grid=(B, S // tm),
        in_specs=[
            tok(D),
            pl.BlockSpec((1, 1, D), lambda b, i: (b, 0, 0)),
            tok(256), tok(256), tok(512), const((1, 256)), const((1, 256)),
            tok(512), tok(256), tok(256),
            pl.BlockSpec((1, tm, 256), lambda b, i: (b, i, 3)),
            const((1, 256)), const((256, 256)),
            const((256, D)), const((512, D)), const((256, D)),
        ],
        out_specs=tok(D),
        out_shape=jax.ShapeDtypeStruct((B, S, D), F32),
        compiler_params=_cparams("parallel", "parallel"),
        name="outproj",
    )(x, g2, yf, yb, pg, ln_g, ln_b, o_mla, rf, rb, zc, gn_g, avg_bd, wa, wb, wc)


def _block_diag2(w):
    z = jnp.zeros_like(w[0])
    return jnp.concatenate([jnp.concatenate([w[0], z], axis=1), jnp.concatenate([z, w[1]], axis=1)], axis=0)


def _pad_inproj(w):
    D = w.shape[0]
    o = RWKV_COLS + MLA_Q_RANK + MLA_KV_RANK
    return jnp.concatenate(
        [w[:, :o], jnp.zeros((D, 64), w.dtype), w[:, o:o + MLA_ROPE], jnp.zeros((D, 32), w.dtype), w[:, o + MLA_ROPE:]],
        axis=1)


def _pad_wq(w):
    r = w.shape[0]
    w = w.reshape(r, MLA_HEADS, MLA_NOPE + MLA_ROPE)
    w = jnp.concatenate([w, jnp.zeros((r, MLA_HEADS, 32), w.dtype)], axis=2)
    return w.reshape(r, MLA_HEADS * MLA_HEAD_PAD)


def _split_wkv(w):
    r = w.shape[0]
    w = w.reshape(r, MLA_HEADS, MLA_NOPE + MLA_V)
    wk = jnp.concatenate([w[:, :, :MLA_NOPE], jnp.zeros((r, MLA_HEADS, 64), w.dtype)], axis=2)
    return wk.reshape(r, MLA_HEADS * MLA_HEAD_PAD), w[:, :, MLA_NOPE:].reshape(r, MLA_WIDTH)


def kernel(x, c, positions, w_ada, b_ada, w_ff1_in, w_ff1_out, w_ff2_in, w_ff2_out, w_in, w_out, rwkv_mu, rwkv_w0,
           rwkv_w_up, rwkv_a0, rwkv_a_up, rwkv_g_up, rwkv_k_k, rwkv_k_a, rwkv_r_k, rwkv_ln_g, rwkv_ln_b,
           mla_q_norm_g, mla_w_uq, mla_kv_norm_g, mla_w_ukv, ret_log_rate, ret_gn_g, final_norm_g):
    B, S, D = x.shape
    L = w_ada.shape[0]
    mod = _mod_call(c, w_ada, b_ada)
    tabs = _rope_call(positions)

    head_id = jnp.arange(256) // HEAD_DIM
    same_head = (head_id[:, None] == head_id[None, :]).astype(F32)
    ones_bd = same_head
    avg_bd = same_head / HEAD_DIM
    final_g = final_norm_g.reshape(1, D)

    for l in range(L):
        m = [mod[l, :, i * D:(i + 1) * D].reshape(B, 1, D) for i in range(N_MOD)]
        sh1, sc1, g1, sh2, sc2, g2, sh3, sc3, g3 = m

        x = _ffn_call(x, sh1, sc1, g1, w_ff1_in[l].astype(BF16), w_ff1_out[l].astype(BF16), final_g, False)

        za, zb, zc = _inproj_call(x, sh2, sc2, _pad_inproj(w_in[l]).astype(BF16), tabs)

        pc, pd, pg = _rwkv_prep_call(
            za, rwkv_mu[l].reshape(1, -1), rwkv_w0[l].reshape(1, 512), _block_diag2(rwkv_w_up[l]),
            rwkv_a0[l].reshape(1, 512), _block_diag2(rwkv_a_up[l]), rwkv_g_up[l],
            rwkv_k_k[l].reshape(1, 256), rwkv_k_a[l].reshape(1, 256), rwkv_r_k[l].reshape(1, 256), ones_bd)
        yf, yb = _rwkv_scan_call(pc, pd)

        wk, wv = _split_wkv(mla_w_ukv[l])
        qt, k, vt = _mla_prep_call(zb, tabs, mla_q_norm_g[l].reshape(1, -1), _pad_wq(mla_w_uq[l]).astype(BF16),
                                   mla_kv_norm_g[l].reshape(1, -1), wk.astype(BF16), wv.T.astype(BF16))
        o_mla = _mla_attn_call(qt, k, vt)

        lr = ret_log_rate[l]
        lr_vec = jnp.repeat(lr, HEAD_DIM, axis=1)
        lr_heads = jnp.broadcast_to(lr.reshape(8, 1), (8, 128))
        rf, rb = _ret_call(zc, lr_vec, lr_heads)

        wo = w_out[l].astype(BF16)
        x = _outproj_call(x, g2, yf, yb, pg, rwkv_ln_g[l].reshape(1, 256), rwkv_ln_b[l].reshape(1, 256), o_mla,
                          rf, rb, zc, ret_gn_g[l].reshape(1, 256), avg_bd,
                          wo[0:256], wo[256:768], wo[768:1024])

        x = _ffn_call(x, sh3, sc3, g3, w_ff2_in[l].astype(BF16), w_ff2_out[l].astype(BF16), final_g, l == L - 1)
    return x
```

```python
import functools
import math

import jax
import jax.numpy as jnp
from jax import lax
from jax.experimental import pallas as pl
from jax.experimental.pallas import tpu as pltpu

F32 = jnp.float32
BF16 = jnp.bfloat16
HIGHEST = lax.Precision.HIGHEST

D_MODEL = 1024
HEAD_DIM = 64
RWKV_WIDTH = 256
RWKV_HEADS = 4
RWKV_LORA = 64
RWKV_GATE_LORA = 128
RWKV_COLS = 1152
MLA_HEADS = 8
MLA_NOPE = 64
MLA_ROPE = 32
MLA_V = 64
MLA_Q_RANK = 384
MLA_KV_RANK = 256
MLA_WIDTH = 512
MLA_HEAD_PAD = 128
MLA_TQ = 2048
MLA_QCOLS = 512
MLA_TK = 512
MLA_ONES_ROWS = 16
RET_WIDTH = 256
RET_HEADS = 4
RET_CHUNK = 128
RET_BATCH_PER_STEP = 4
RWKV_CHUNK = 64
RWKV_BATCH_PER_STEP = 4
RWKV_PASSES = 1
D_FF = 2816
N_MOD = 9
ROPE_BASE = 10000.0
NORM_EPS = 1e-6
RWKV_LN_EPS = 64e-5
RET_LN_EPS = 1e-5

VMEM_LIMIT = 48 * 1024 * 1024


def _cparams(*sem):
    return pltpu.CompilerParams(dimension_semantics=sem, vmem_limit_bytes=VMEM_LIMIT)


def _sigmoid(x):
    return 1.0 / (1.0 + jnp.exp(-x))


def _dot(a, b):
    return jnp.dot(a, b, preferred_element_type=F32)


def _dot_hi(a, b):
    return jnp.dot(a, b, precision=HIGHEST, preferred_element_type=F32)


def _split(a):
    hi = a.astype(BF16)
    lo = (a - hi.astype(F32)).astype(BF16)
    return hi, lo


def _dg3(a, b, dims):
    ah, al = _split(a)
    bh, bl = _split(b)
    dn = (dims, ((), ()))
    out = lax.dot_general(ah, bh, dn, preferred_element_type=F32)
    out = out + lax.dot_general(ah, bl, dn, preferred_element_type=F32)
    out = out + lax.dot_general(al, bh, dn, preferred_element_type=F32)
    return out


_NN = ((1,), (0,))
_NT = ((1,), (1,))
_TN = ((0,), (0,))


def _rms_mod(x, sh, sc):
    ms = jnp.mean(x * x, axis=-1, keepdims=True)
    return (x * lax.rsqrt(ms + NORM_EPS)) * (1.0 + sc) + sh


def _mod_kernel(c_ref, w_ref, b_ref, o_ref):
    c = c_ref[...]
    cond = c * _sigmoid(c)
    o_ref[0] = _dot_hi(cond, w_ref[0]) + b_ref[0]


def _mod_call(c, w_ada, b_ada):
    L, D, N = w_ada.shape
    B = c.shape[0]
    tn = 1152
    return pl.pallas_call(
        _mod_kernel,
        grid=(L, N // tn),
        in_specs=[
            pl.BlockSpec((B, D), lambda l, j: (0, 0)),
            pl.BlockSpec((1, D, tn), lambda l, j: (l, 0, j)),
            pl.BlockSpec((1, 1, tn), lambda l, j: (l, 0, j)),
        ],
        out_specs=pl.BlockSpec((1, B, tn), lambda l, j: (l, 0, j)),
        out_shape=jax.ShapeDtypeStruct((L, B, N), F32),
        compiler_params=_cparams("parallel", "parallel"),
        name="adaln_mod",
    )(c, w_ada, b_ada.reshape(L, 1, N))


def _rope_kernel(pos_ref, inv_ref, mc_ref, m1_ref, ma_ref, mb_ref, o_ref):
    pos = pos_ref[0].astype(F32)
    for i in range(2):
        ang = pos * inv_ref[i:i + 1, :]
        cos = jnp.cos(ang)
        sin = jnp.sin(ang)
        o_ref[i, 0, :, 0:128] = cos * mc_ref[i:i + 1, :] + m1_ref[i:i + 1, :]
        o_ref[i, 0, :, 128:256] = sin * ma_ref[i:i + 1, :]
        o_ref[i, 0, :, 256:384] = sin * mb_ref[i:i + 1, :]


def _rope_consts():
    lane = jnp.arange(128)
    inv_m = ROPE_BASE ** (-jnp.arange(0, MLA_ROPE, 2, dtype=F32) / MLA_ROPE)
    in_rope = (lane >= 64) & (lane < 96)
    inv0 = jnp.where(in_rope, inv_m[(lane - 64) % 16], 0.0)
    mc0 = in_rope.astype(F32)
    m10 = (lane < 64).astype(F32)
    ma0 = jnp.where((lane >= 64) & (lane < 80), -1.0, 0.0)
    mb0 = jnp.where((lane >= 80) & (lane < 96), 1.0, 0.0)
    inv_r = ROPE_BASE ** (-jnp.arange(0, HEAD_DIM, 2, dtype=F32) / HEAD_DIM)
    inv1 = inv_r[lane % 32]
    mc1 = jnp.ones((128,), F32)
    m11 = jnp.zeros((128,), F32)
    ma1 = jnp.where((lane % 64) < 32, -1.0, 0.0)
    mb1 = jnp.where((lane % 64) >= 32, 1.0, 0.0)
    st = lambda a, b: jnp.stack([a, b]).astype(F32)
    return st(inv0, inv1), st(mc0, mc1), st(m10, m11), st(ma0, ma1), st(mb0, mb1)


def _rope_call(positions):
    B, S = positions.shape
    ts = min(S, 512)
    consts = _rope_consts()
    cspec = pl.BlockSpec((2, 128), lambda b, i: (0, 0))
    return pl.pallas_call(
        _rope_kernel,
        grid=(B, S // ts),
        in_specs=[pl.BlockSpec((1, ts, 1), lambda b, i: (b, i, 0))] + [cspec] * 5,
        out_specs=pl.BlockSpec((2, 1, ts, 384), lambda b, i: (0, b, i, 0)),
        out_shape=jax.ShapeDtypeStruct((2, B, S, 384), F32),
        compiler_params=_cparams("parallel", "parallel"),
        name="rope_tables",
    )(positions.reshape(B, S, 1), *consts)


def _apply_rope(x, tab, shift):
    n = x.shape[1] // 128
    width = x.shape[1]
    cc = jnp.tile(tab[:, 0:128], (1, n))
    sa = jnp.tile(tab[:, 128:256], (1, n))
    sb = jnp.tile(tab[:, 256:384], (1, n))
    return x * cc + pltpu.roll(x, width - shift, axis=1) * sa + pltpu.roll(x, shift, axis=1) * sb


def _ffn_kernel(x_ref, sh_ref, sc_ref, g_ref, wg_ref, wu_ref, wo_ref, fg_ref, o_ref, h_scr, acc_scr, *, nj, final):
    j = pl.program_id(2)

    @pl.when(j == 0)
    def _():
        h = _rms_mod(x_ref[0], sh_ref[0], sc_ref[0])
        h_scr[...] = h.astype(BF16)
        acc_scr[...] = jnp.zeros_like(acc_scr)

    h = h_scr[...]
    gate = _dot(h, wg_ref[...])
    up = _dot(h, wu_ref[...])
    act = (gate * _sigmoid(gate) * up).astype(BF16)
    acc_scr[...] += _dot(act, wo_ref[...])

    @pl.when(j == nj - 1)
    def _():
        y = x_ref[0] + (0.5 * g_ref[0]) * acc_scr[...]
        if final:
            ms = jnp.mean(y * y, axis=-1, keepdims=True)
            y = (y * lax.rsqrt(ms + NORM_EPS)) * fg_ref[...]
        o_ref[0] = y


def _ffn_call(x, sh, sc, g, w_in_bf, w_out_bf, final_g, final):
    B, S, D = x.shape
    tm = min(S, 512)
    tf = 1408
    nj = D_FF // tf
    vec = pl.BlockSpec((1, 1, D), lambda b, i, j: (b, 0, 0))
    return pl.pallas_call(
        functools.partial(_ffn_kernel, nj=nj, final=final),
        grid=(B, S // tm, nj),
        in_specs=[
            pl.BlockSpec((1, tm, D), lambda b, i, j: (b, i, 0)),
            vec, vec, vec,
            pl.BlockSpec((D, tf), lambda b, i, j: (0, j)),
            pl.BlockSpec((D, tf), lambda b, i, j: (0, j + nj)),
            pl.BlockSpec((tf, D), lambda b, i, j: (j, 0)),
            pl.BlockSpec((1, D), lambda b, i, j: (0, 0)),
        ],
        out_specs=pl.BlockSpec((1, tm, D), lambda b, i, j: (b, i, 0)),
        out_shape=jax.ShapeDtypeStruct((B, S, D), F32),
        scratch_shapes=[pltpu.VMEM((tm, D), BF16), pltpu.VMEM((tm, D), F32)],
        compiler_params=_cparams("parallel", "parallel", "arbitrary"),
        name="ffn",
    )(x, sh, sc, g, w_in_bf, w_in_bf, w_out_bf, final_g)


ZB_COLS = MLA_Q_RANK + MLA_KV_RANK + MLA_HEAD_PAD
ZC_COLS = 4 * RET_WIDTH
Z_COLS = RWKV_COLS + ZB_COLS + ZC_COLS


def _inproj_kernel(x_ref, sh_ref, sc_ref, w_ref, tab_ref, za_ref, zb_ref, zc_ref):
    h = _rms_mod(x_ref[0], sh_ref[0], sc_ref[0]).astype(BF16)
    z = _dot(h, w_ref[...])
    za_ref[0] = z[:, 0:RWKV_COLS]
    zb_ref[0] = z[:, RWKV_COLS:RWKV_COLS + ZB_COLS].astype(BF16)
    o = RWKV_COLS + ZB_COLS
    tab = tab_ref[0, 0]
    zc_ref[0, :, 0:256] = _apply_rope(z[:, o:o + 256], tab, 32).astype(BF16)
    zc_ref[0, :, 256:512] = (_apply_rope(z[:, o + 256:o + 512], tab, 32) * (HEAD_DIM ** -0.5)).astype(BF16)
    zc_ref[0, :, 512:1024] = z[:, o + 512:Z_COLS].astype(BF16)


def _inproj_call(x, sh, sc, wz_bf, tabs):
    B, S, D = x.shape
    tm = min(S, 512)
    vec = pl.BlockSpec((1, 1, D), lambda b, i: (b, 0, 0))
    out = lambda n: pl.BlockSpec((1, tm, n), lambda b, i: (b, i, 0))
    return pl.pallas_call(
        _inproj_kernel,
        grid=(B, S // tm),
        in_specs=[
            pl.BlockSpec((1, tm, D), lambda b, i: (b, i, 0)),
            vec, vec,
            pl.BlockSpec((D, Z_COLS), lambda b, i: (0, 0)),
            pl.BlockSpec((1, 1, tm, 384), lambda b, i: (1, b, i, 0)),
        ],
        out_specs=[out(RWKV_COLS), out(ZB_COLS), out(ZC_COLS)],
        out_shape=[jax.ShapeDtypeStruct((B, S, n), dt)
                   for n, dt in ((RWKV_COLS, F32), (ZB_COLS, BF16), (ZC_COLS, BF16))],
        compiler_params=_cparams("parallel", "parallel"),
        name="inproj",
    )(x, sh, sc, wz_bf, tabs)


def _rwkv_prep_kernel(z_ref, zp_ref, zn_ref, mu_ref, w0_ref, wup_ref, a0_ref, aup_ref, gup_ref,
                      kk_ref, ka_ref, rk_ref, ones_ref, pc_ref, pd_ref, pw_ref, pg_ref, *, ts, nt):
    i = pl.program_id(1)
    z = z_ref[0]
    row = lax.broadcasted_iota(jnp.int32, (ts, 1), 0)
    prev_edge = jnp.where(i > 0, zp_ref[0, 7:8, :], 0.0)
    next_edge = jnp.where(i < nt - 1, zn_ref[0, 0:1, :], 0.0)
    prev = jnp.where(row == 0, prev_edge, pltpu.roll(z, 1, axis=0))
    nxt = jnp.where(row == ts - 1, next_edge, pltpu.roll(z, ts - 1, axis=0))
    zs = z + mu_ref[...] * (0.5 * (prev + nxt) - z)

    r = zs[:, 0:256]
    k = zs[:, 256:512]
    v = zs[:, 512:768]
    g_lo = zs[:, 768:896]
    w_lo = zs[:, 896:1024]
    a_lo = zs[:, 1024:1152]

    w_raw = w0_ref[...] + _dg3(jnp.tanh(w_lo), wup_ref[...], _NN)
    logw = (-math.exp(-0.5)) * _sigmoid(w_raw)
    a = _sigmoid(a0_ref[...] + _dg3(a_lo, aup_ref[...], _NN))
    g = _dg3(_sigmoid(g_lo), gup_ref[...], _NN)

    ones_bd = ones_ref[...]
    kk0 = k * kk_ref[...]
    nrm = jnp.sqrt(_dot_hi(kk0 * kk0, ones_bd))
    kk = kk0 / jnp.maximum(nrm, 1e-12)
    bonus = _dot_hi(r * k * rk_ref[...], ones_bd) * v

    pc_ref[0, :, 0:256] = r.astype(BF16)
    pc_ref[0, :, 256:512] = v.astype(BF16)
    pc_ref[0, :, 512:768] = kk.astype(BF16)
    for d in range(2):
        a_d = a[:, 256 * d:256 * d + 256]
        pd_ref[d, 0, :, 0:256] = (k * (1.0 + (a_d - 1.0) * ka_ref[...])).astype(BF16)
        pd_ref[d, 0, :, 256:512] = (kk * a_d).astype(BF16)
        pw_ref[d, 0] = logw[:, 256 * d:256 * d + 256]
    pg_ref[0, :, 0:256] = g.astype(BF16)
    pg_ref[0, :, 256:512] = bonus.astype(BF16)


def _rwkv_prep_call(za, mu, w0, wup_bd, a0, aup_bd, gup, k_k, k_a, r_k, ones_bd):
    B, S, _ = za.shape
    ts = min(S, 512)
    nt = S // ts
    nb8 = S // 8
    const = lambda shape: pl.BlockSpec(shape, lambda b, i: (0,) * len(shape))
    return pl.pallas_call(
        functools.partial(_rwkv_prep_kernel, ts=ts, nt=nt),
        grid=(B, nt),
        in_specs=[
            pl.BlockSpec((1, ts, RWKV_COLS), lambda b, i: (b, i, 0)),
            pl.BlockSpec((1, 8, RWKV_COLS), lambda b, i: (b, jnp.maximum(i * (ts // 8) - 1, 0), 0)),
            pl.BlockSpec((1, 8, RWKV_COLS), lambda b, i: (b, jnp.minimum((i + 1) * (ts // 8), nb8 - 1), 0)),
            const((1, RWKV_COLS)),
            const((1, 512)), const((128, 512)), const((1, 512)), const((128, 512)), const((128, 256)),
            const((1, 256)), const((1, 256)), const((1, 256)), const((256, 256)),
        ],
        out_specs=[
            pl.BlockSpec((1, ts, 768), lambda b, i: (b, i, 0)),
            pl.BlockSpec((2, 1, ts, 512), lambda b, i: (0, b, i, 0)),
            pl.BlockSpec((2, 1, ts, 256), lambda b, i: (0, b, i, 0)),
            pl.BlockSpec((1, ts, 512), lambda b, i: (b, i, 0)),
        ],
        out_shape=[
            jax.ShapeDtypeStruct((B, S, 768), BF16),
            jax.ShapeDtypeStruct((2, B, S, 512), BF16),
            jax.ShapeDtypeStruct((2, B, S, 256), F32),
            jax.ShapeDtypeStruct((B, S, 512), BF16),
        ],
        compiler_params=_cparams("parallel", "parallel"),
        name="rwkv_prep",
    )(za, za, za, mu, w0, wup_bd, a0, aup_bd, gup, k_k, k_a, r_k, ones_bd)


def _rwkv_dir_operands(direction, pc, pd, logw):
    C = RWKV_CHUNK
    r = pc[:, 0:256]
    v = pc[:, 256:512]
    kk = pc[:, 512:768]
    kmod = pd[:, 0:256]
    kka = pd[:, 256:512]

    row = lax.broadcasted_iota(jnp.int32, (C, C), 0)
    col = lax.broadcasted_iota(jnp.int32, (C, C), 1)
    tri = ((col <= row) if direction == 0 else (col >= row)).astype(F32)
    l_in = _dot_hi(tri, logw)
    l_ex = l_in - logw
    l_tot = l_in[C - 1:C, :] if direction == 0 else l_in[0:1, :]
    e_ex = jnp.exp(l_ex)
    e_neg = jnp.exp(-l_in)
    e_tot = jnp.exp(l_tot - l_in)
    kkg = kk * e_ex
    bbar = kka * e_neg
    kbar = kmod * e_neg
    bg = kka * e_tot
    kg = kmod * e_tot
    rg = r * jnp.exp(l_in) if direction == 0 else r * e_ex
    g_tot = jnp.exp(l_tot)
    return dict(kkg=kkg, bbar=bbar, kbar=kbar, bg=bg, kg=kg, rg=rg, v=v, g_tot=g_tot)


def _operand(a, passes):
    hi = a.astype(BF16)
    lo = (a - hi.astype(F32)).astype(BF16) if passes > 1 else None
    return hi, lo


def _mm(a, b, dims):
    dn = (dims, ((), ()))
    out = lax.dot_general(a[0], b[0], dn, preferred_element_type=F32)
    if b[1] is not None:
        out = out + lax.dot_general(a[0], b[1], dn, preferred_element_type=F32)
    if a[1] is not None:
        out = out + lax.dot_general(a[1], b[0], dn, preferred_element_type=F32)
    return out


def _rwkv_scan_kernel(pcf_ref, pcb_ref, pdf_ref, pdb_ref, pwf_ref, pwb_ref, yf_ref, yb_ref, s_ref):
    C = RWKV_CHUNK
    P = RWKV_PASSES

    @pl.when(pl.program_id(1) == 0)
    def _():
        s_ref[...] = jnp.zeros_like(s_ref)

    row = lax.broadcasted_iota(jnp.int32, (C, C), 0)
    col = lax.broadcasted_iota(jnp.int32, (C, C), 1)
    eye = col == row
    ident = jnp.where(eye, 1.0, 0.0)
    nb = pcf_ref.shape[0]
    ops = {}
    for bb in range(nb):
        ops[bb, 0] = _rwkv_dir_operands(0, pcf_ref[bb].astype(F32), pdf_ref[0, bb].astype(F32), pwf_ref[0, bb])
        ops[bb, 1] = _rwkv_dir_operands(1, pcb_ref[bb].astype(F32), pdb_ref[0, bb].astype(F32), pwb_ref[0, bb])
    earlier = (col < row, col > row)
    ymask = (col <= row, col > row)
    units = [(bb, d, h) for bb in range(nb) for d in range(2) for h in range(RWKV_HEADS)]
    nu = len(units)
    sl = lambda h: slice(64 * h, 64 * h + 64)
    s0 = [s_ref[u] for u in units]

    def head(name):
        return [_operand(ops[bb, d][name][:, sl(h)], P) for bb, d, h in units]

    kkg, bbar, kbar, bg, kg, rg, v = (head(n) for n in ("kkg", "bbar", "kbar", "bg", "kg", "rg", "v"))
    akk = [jnp.where(earlier[d], _mm(kkg[i], bbar[i], _NT), 0.0) for i, (_, d, _) in enumerate(units)]
    bk = [jnp.where(earlier[d], _mm(kkg[i], kbar[i], _NT), 0.0) for i, (_, d, _) in enumerate(units)]
    ark = [jnp.where(ymask[d], _mm(rg[i], bbar[i], _NT), 0.0) for i, (_, d, _) in enumerate(units)]
    brk = [jnp.where(ymask[d], _mm(rg[i], kbar[i], _NT), 0.0) for i, (_, d, _) in enumerate(units)]
    pw = [-a for a in akk]
    t_inv = [ident + x for x in pw]
    for _ in range(int(math.log2(C)) - 1):
        pw_o = [_operand(x, P) for x in pw]
        pw = [_mm(x, x, _NN) for x in pw_o]
        pw_o = [_operand(x, P) for x in pw]
        t_inv = [t + _mm(_operand(t, P), x, _NN) for t, x in zip(t_inv, pw_o)]
    t_o = [_operand(t, P) for t in t_inv]
    bkv = [_mm(_operand(b, P), x, _NN) for b, x in zip(bk, v)]
    w_m = [_mm(t, x, _NN) for t, x in zip(t_o, kkg)]
    u_m = [_mm(t, _operand(x, P), _NN) for t, x in zip(t_o, bkv)]
    w_o = [_operand(x, P) for x in w_m]
    u_o = [_operand(x, P) for x in u_m]
    ark_o = [_operand(x, P) for x in ark]
    brk_o = [_operand(x, P) for x in brk]
    m_m = [jnp.where(eye, ops[bb, d]["g_tot"][:, sl(h)], 0.0) - _mm(w_o[i], bg[i], _TN)
           for i, (bb, d, h) in enumerate(units)]
    n_m = [_mm(v[i], kg[i], _TN) - _mm(u_o[i], bg[i], _TN) for i in range(nu)]
    q_m = [ops[bb, d]["rg"][:, sl(h)] - _mm(ark_o[i], w_o[i], _NN) for i, (bb, d, h) in enumerate(units)]
    y_i = [_mm(brk_o[i], v[i], _NN) - _mm(ark_o[i], u_o[i], _NN) for i in range(nu)]
    s_o = [_operand(s, P) for s in s0]
    y = [y_i[i] + _mm(_operand(q_m[i], P), s_o[i], _NT) for i in range(nu)]
    s_new = [_mm(s_o[i], _operand(m_m[i], P), _NN) + n_m[i] for i in range(nu)]
    for i, (bb, d, h) in enumerate(units):
        (yf_ref if d == 0 else yb_ref)[bb, :, sl(h)] = y[i]
        s_ref[bb, d, h] = s_new[i]


def _rwkv_scan_call(pc, pd, pw):
    B, S, _ = pc.shape
    C = RWKV_CHUNK
    nc = S // C
    nb = math.gcd(B, RWKV_BATCH_PER_STEP)
    return pl.pallas_call(
        _rwkv_scan_kernel,
        grid=(B // nb, nc),
        in_specs=[
            pl.BlockSpec((nb, C, 768), lambda b, c: (b, c, 0)),
            pl.BlockSpec((nb, C, 768), lambda b, c: (b, nc - 1 - c, 0)),
            pl.BlockSpec((1, nb, C, 512), lambda b, c: (0, b, c, 0)),
            pl.BlockSpec((1, nb, C, 512), lambda b, c: (1, b, nc - 1 - c, 0)),
            pl.BlockSpec((1, nb, C, 256), lambda b, c: (0, b, c, 0)),
            pl.BlockSpec((1, nb, C, 256), lambda b, c: (1, b, nc - 1 - c, 0)),
        ],
        out_specs=[
            pl.BlockSpec((nb, C, 256), lambda b, c: (b, c, 0)),
            pl.BlockSpec((nb, C, 256), lambda b, c: (b, nc - 1 - c, 0)),
        ],
        out_shape=[jax.ShapeDtypeStruct((B, S, 256), F32)] * 2,
        scratch_shapes=[pltpu.VMEM((nb, 2, RWKV_HEADS, HEAD_DIM, HEAD_DIM), F32)],
        compiler_params=_cparams("parallel", "arbitrary"),
        name="rwkv_scan",
    )(pc, pc, pd, pd, pw, pw)


def _mla_prep_kernel(zb_ref, tab_ref, qg_ref, wq_ref, kg_ref, wk_ref, wvt_ref, qt_ref, k_ref, vt_ref):
    zb = zb_ref[0].astype(F32)
    tab = tab_ref[0, 0]
    cq = zb[:, 0:MLA_Q_RANK]
    ckv = zb[:, MLA_Q_RANK:MLA_Q_RANK + MLA_KV_RANK]
    kpe = zb[:, MLA_Q_RANK + MLA_KV_RANK:ZB_COLS]

    def rms(t):
        return t * lax.rsqrt(jnp.mean(t * t, axis=-1, keepdims=True) + NORM_EPS)

    cqn = (rms(cq) * qg_ref[...]).astype(BF16)
    ckvn = (rms(ckv) * kg_ref[...]).astype(BF16)
    q = _dot(cqn, wq_ref[...])
    q = _apply_rope(q, tab, 16) * (math.log2(math.e) * (MLA_NOPE + MLA_ROPE) ** -0.5)
    k = _dot(ckvn, wk_ref[...])
    kpe = _apply_rope(kpe, tab, 16)
    k = k + jnp.tile(kpe, (1, MLA_HEADS))
    qt_ref[0] = q.T.astype(BF16)
    k_ref[0] = k.astype(BF16)
    vt_ref[0, 0] = lax.dot_general(wvt_ref[...], ckvn, (_NT, ((), ())), preferred_element_type=F32).astype(BF16)


def _mla_key_tile(S):
    return min(S, MLA_TK)


def _mla_prep_call(zb, tabs, qg, wq_bf, kg, wk_bf, wvt_bf):
    B, S, _ = zb.shape
    ts = _mla_key_tile(S)
    HP = MLA_HEADS * MLA_HEAD_PAD
    const = lambda shape: pl.BlockSpec(shape, lambda b, i: (0,) * len(shape))
    return pl.pallas_call(
        _mla_prep_kernel,
        grid=(B, S // ts),
        in_specs=[
            pl.BlockSpec((1, ts, ZB_COLS), lambda b, i: (b, i, 0)),
            pl.BlockSpec((1, 1, ts, 384), lambda b, i: (0, b, i, 0)),
            const((1, MLA_Q_RANK)), const((MLA_Q_RANK, HP)),
            const((1, MLA_KV_RANK)), const((MLA_KV_RANK, HP)), const((MLA_WIDTH, MLA_KV_RANK)),
        ],
        out_specs=[
            pl.BlockSpec((1, HP, ts), lambda b, i: (b, 0, i)),
            pl.BlockSpec((1, ts, HP), lambda b, i: (b, i, 0)),
            pl.BlockSpec((1, 1, MLA_WIDTH, ts), lambda b, i: (b, i, 0, 0)),
        ],
        out_shape=[
            jax.ShapeDtypeStruct((B, HP, S), BF16),
            jax.ShapeDtypeStruct((B, S, HP), BF16),
            jax.ShapeDtypeStruct((B, S // ts, MLA_WIDTH, ts), BF16),
        ],
        compiler_params=_cparams("parallel", "parallel"),
        name="mla_prep",
    )(zb, tabs, qg, wq_bf, kg, wk_bf, wvt_bf)


def _mla_attn_kernel(qt_ref, k_ref, vt_ref, o_ref, *, tk, nk):
    tq = qt_ref.shape[2]
    cw = min(tq, MLA_QCOLS)
    ncol = tq // cw
    chunks = [(hh, c) for hh in range(2) for c in range(ncol)]
    ones = jnp.ones((MLA_ONES_ROWS, tk), BF16)

    def body(i, carry):
        ks = pl.multiple_of(i * tk, tk)
        scores = [_dot(k_ref[0, pl.ds(ks, tk), 128 * hh:128 * hh + 128],
                       qt_ref[0, 128 * hh:128 * hh + 128, c * cw:(c + 1) * cw]) for hh, c in chunks]
        stats = []
        for j, s in enumerate(scores):
            m = carry[j][0]
            m_new = jnp.maximum(m, jnp.max(s, axis=0, keepdims=True))
            stats.append((m_new, jnp.exp2(m - m_new), jnp.exp2(s - m_new).astype(BF16)))
        new = []
        for j, (hh, c) in enumerate(chunks):
            m_new, alpha, p = stats[j]
            vt_ones = jnp.concatenate([vt_ref[0, i, 64 * hh:64 * hh + 64, :], ones], axis=0)
            new.append((m_new, alpha * carry[j][1] + _dot(vt_ones, p)))
        return tuple(new)

    init = (jnp.full((1, cw), -jnp.inf, F32), jnp.zeros((MLA_V + MLA_ONES_ROWS, cw), F32))
    res = lax.fori_loop(0, nk, body, (init,) * len(chunks))
    out_t = [jnp.concatenate([res[hh * ncol + c][1][0:MLA_V] / res[hh * ncol + c][1][MLA_V:MLA_V + 1]
                              for c in range(ncol)], axis=1) for hh in range(2)]
    o_ref[0] = jnp.concatenate(out_t, axis=0).T.astype(BF16)


def _mla_attn_call(qt, k, vt):
    B, S, _ = k.shape
    tq = min(S, MLA_TQ)
    tk = _mla_key_tile(S)
    nk = S // tk
    return pl.pallas_call(
        functools.partial(_mla_attn_kernel, tk=tk, nk=nk),
        grid=(B, MLA_HEADS // 2, S // tq),
        in_specs=[
            pl.BlockSpec((1, 256, tq), lambda b, h, i: (b, h, i)),
            pl.BlockSpec((1, S, 256), lambda b, h, i: (b, 0, h)),
            pl.BlockSpec((1, nk, 128, tk), lambda b, h, i: (b, 0, h, 0)),
        ],
        out_specs=pl.BlockSpec((1, tq, 128), lambda b, h, i: (b, i, h)),
        out_shape=jax.ShapeDtypeStruct((B, S, MLA_WIDTH), BF16),
        compiler_params=_cparams("parallel", "parallel", "arbitrary"),
        name="mla_attn",
    )(qt, k, vt)


def _ret_dir_operands(direction, zc, lrv, lrh):
    C = RET_CHUNK
    q = zc[:, 0:256]
    k = zc[:, 256:512]
    v = zc[:, 512:768]
    lgv = -jnp.exp(lrv)
    lgh = -jnp.exp(lrh)
    pos = lax.broadcasted_iota(jnp.int32, (C, 1), 0).astype(F32)
    row = lax.broadcasted_iota(jnp.int32, (C, C), 0)
    col = lax.broadcasted_iota(jnp.int32, (C, C), 1)
    if direction == 0:
        kw = k * jnp.exp(lgv * (C - 1.0 - pos))
        qw = q * jnp.exp(lgv * (pos + 1.0))
        mask = col <= row
        dist = (row - col).astype(F32)
    else:
        kw = k * jnp.exp(lgv * pos)
        qw = q * jnp.exp(lgv * (C - pos))
        mask = col > row
        dist = (col - row).astype(F32)
    dist = jnp.maximum(dist, 0.0)
    dmats = [jnp.where(mask, jnp.exp(lgh[h:h + 1, :] * dist), 0.0) for h in range(RET_HEADS)]
    decays = [jnp.exp(lgh[h:h + 1, 0:64] * C) for h in range(RET_HEADS)]
    qb, kb, vb, kwb, qwb = (t.astype(BF16) for t in (q, k, v, kw, qw))
    return dict(q=qb, k=kb, v=vb, kw=kwb, qw=qwb, dmat=dmats, decay=decays)


def _ret_kernel(zf_ref, zb_ref, lrv_ref, lrh_ref, yf_ref, yb_ref, r_ref):
    @pl.when(pl.program_id(1) == 0)
    def _():
        r_ref[...] = jnp.zeros_like(r_ref)

    nb = zf_ref.shape[0]
    ops = {}
    for bb in range(nb):
        ops[bb, 0] = _ret_dir_operands(0, zf_ref[bb].astype(F32), lrv_ref[0:1, :], lrh_ref[0:4, :])
        ops[bb, 1] = _ret_dir_operands(1, zb_ref[bb].astype(F32), lrv_ref[1:2, :], lrh_ref[4:8, :])
    units = [(bb, d, h) for bb in range(nb) for d in range(2) for h in range(RET_HEADS)]
    sl = lambda h: slice(64 * h, 64 * h + 64)
    r0 = [r_ref[u] for u in units]
    sc = [lax.dot_general(ops[bb, d]["q"][:, sl(h)], ops[bb, d]["k"][:, sl(h)], (_NT, ((), ())),
                          preferred_element_type=F32) * ops[bb, d]["dmat"][h] for bb, d, h in units]
    kv = [lax.dot_general(ops[bb, d]["kw"][:, sl(h)], ops[bb, d]["v"][:, sl(h)], (_TN, ((), ())),
                          preferred_element_type=F32) for bb, d, h in units]
    cross = [_dot(ops[bb, d]["qw"][:, sl(h)], r0[i].astype(BF16)) for i, (bb, d, h) in enumerate(units)]
    inner = [_dot(sc[i].astype(BF16), ops[bb, d]["v"][:, sl(h)]) for i, (bb, d, h) in enumerate(units)]
    for i, (bb, d, h) in enumerate(units):
        (yf_ref if d == 0 else yb_ref)[bb, :, sl(h)] = inner[i] + cross[i]
        r_ref[bb, d, h] = r0[i] * ops[bb, d]["decay"][h] + kv[i]


def _ret_call(zc, lr_vec, lr_heads):
    B, S, _ = zc.shape
    C = RET_CHUNK
    nc = S // C
    nb = math.gcd(B, RET_BATCH_PER_STEP)
    const = lambda shape: pl.BlockSpec(shape, lambda b, c: (0,) * len(shape))
    return pl.pallas_call(
        _ret_kernel,
        grid=(B // nb, nc),
        in_specs=[
            pl.BlockSpec((nb, C, ZC_COLS), lambda b, c: (b, c, 0)),
            pl.BlockSpec((nb, C, ZC_COLS), lambda b, c: (b, nc - 1 - c, 0)),
            const((2, 256)), const((8, 128)),
        ],
        out_specs=[
            pl.BlockSpec((nb, C, 256), lambda b, c: (b, c, 0)),
            pl.BlockSpec((nb, C, 256), lambda b, c: (b, nc - 1 - c, 0)),
        ],
        out_shape=[jax.ShapeDtypeStruct((B, S, 256), F32)] * 2,
        scratch_shapes=[pltpu.VMEM((nb, 2, RET_HEADS, HEAD_DIM, HEAD_DIM), F32)],
        compiler_params=_cparams("parallel", "arbitrary"),
        name="retention",
    )(zc, zc, lr_vec, lr_heads)


def _head_norm(y, avg_bd, eps):
    mu = _dot_hi(y, avg_bd)
    d = y - mu
    var = _dot_hi(d * d, avg_bd)
    return d * lax.rsqrt(var + eps)


def _outproj_kernel(x_ref, g2_ref, yf_ref, yb_ref, pg_ref, lng_ref, lnb_ref, om_ref, rf_ref, rb_ref,
                    gate_ref, gng_ref, avg_ref, wa_ref, wb_ref, wc_ref, o_ref):
    avg_bd = avg_ref[...]
    y = _head_norm(yf_ref[0] + yb_ref[0], avg_bd, RWKV_LN_EPS) * lng_ref[...] + lnb_ref[...]
    pg = pg_ref[0].astype(F32)
    o_a = (y + pg[:, 256:512]) * pg[:, 0:256]
    yr = _head_norm(rf_ref[0] + rb_ref[0], avg_bd, RET_LN_EPS) * gng_ref[...]
    gate = gate_ref[0].astype(F32)
    o_c = (gate * _sigmoid(gate)) * yr
    mixed = _dot(o_a.astype(BF16), wa_ref[...])
    mixed = mixed + _dot(om_ref[0], wb_ref[...])
    mixed = mixed + _dot(o_c.astype(BF16), wc_ref[...])
    o_ref[0] = x_ref[0] + g2_ref[0] * mixed


def _outproj_call(x, g2, yf, yb, pg, ln_g, ln_b, o_mla, rf, rb, zc, gn_g, avg_bd, wa, wb, wc):
    B, S, D = x.shape
    tm = min(S, 512)
    tok = lambda n: pl.BlockSpec((1, tm, n), lambda b, i: (b, i, 0))
    const = lambda shape: pl.BlockSpec(shape, lambda b, i: (0,) * len(shape))
    return pl.pallas_call(
        _outproj_kernel,
        grid=(B, S // tm),
        in_specs=[
            tok(D),
            pl.BlockSpec((1, 1, D), lambda b, i: (b, 0, 0)),
            tok(256), tok(256), tok(512), const((1, 256)), const((1, 256)),
            tok(512), tok(256), tok(256),
            pl.BlockSpec((1, tm, 256), lambda b, i: (b, i, 3)),
            const((1, 256)), const((256, 256)),
            const((256, D)), const((512, D)), const((256, D)),
        ],
        out_specs=tok(D),
        out_shape=jax.ShapeDtypeStruct((B, S, D), F32),
        compiler_params=_cparams("parallel", "parallel"),
        name="outproj",
    )(x, g2, yf, yb, pg, ln_g, ln_b, o_mla, rf, rb, zc, gn_g, avg_bd, wa, wb, wc)


def _block_diag2(w):
    z = jnp.zeros_like(w[0])
    return jnp.concatenate([jnp.concatenate([w[0], z], axis=1), jnp.concatenate([z, w[1]], axis=1)], axis=0)


def _pad_inproj(w):
    D = w.shape[0]
    o = RWKV_COLS + MLA_Q_RANK + MLA_KV_RANK
    return jnp.concatenate(
        [w[:, :o], jnp.zeros((D, 64), w.dtype), w[:, o:o + MLA_ROPE], jnp.zeros((D, 32), w.dtype), w[:, o + MLA_ROPE:]],
        axis=1)


def _pad_wq(w):
    r = w.shape[0]
    w = w.reshape(r, MLA_HEADS, MLA_NOPE + MLA_ROPE)
    w = jnp.concatenate([w, jnp.zeros((r, MLA_HEADS, 32), w.dtype)], axis=2)
    return w.reshape(r, MLA_HEADS * MLA_HEAD_PAD)


def _split_wkv(w):
    r = w.shape[0]
    w = w.reshape(r, MLA_HEADS, MLA_NOPE + MLA_V)
    wk = jnp.concatenate([w[:, :, :MLA_NOPE], jnp.zeros((r, MLA_HEADS, 64), w.dtype)], axis=2)
    return wk.reshape(r, MLA_HEADS * MLA_HEAD_PAD), w[:, :, MLA_NOPE:].reshape(r, MLA_WIDTH)


def kernel(x, c, positions, w_ada, b_ada, w_ff1_in, w_ff1_out, w_ff2_in, w_ff2_out, w_in, w_out, rwkv_mu, rwkv_w0,
           rwkv_w_up, rwkv_a0, rwkv_a_up, rwkv_g_up, rwkv_k_k, rwkv_k_a, rwkv_r_k, rwkv_ln_g, rwkv_ln_b,
           mla_q_norm_g, mla_w_uq, mla_kv_norm_g, mla_w_ukv, ret_log_rate, ret_gn_g, final_norm_g):
    B, S, D = x.shape
    L = w_ada.shape[0]
    mod = _mod_call(c, w_ada, b_ada)
    tabs = _rope_call(positions)

    head_id = jnp.arange(256) // HEAD_DIM
    same_head = (head_id[:, None] == head_id[None, :]).astype(F32)
    ones_bd = same_head
    avg_bd = same_head / HEAD_DIM
    final_g = final_norm_g.reshape(1, D)

    for l in range(L):
        m = [mod[l, :, i * D:(i + 1) * D].reshape(B, 1, D) for i in range(N_MOD)]
        sh1, sc1, g1, sh2, sc2, g2, sh3, sc3, g3 = m

        x = _ffn_call(x, sh1, sc1, g1, w_ff1_in[l].astype(BF16), w_ff1_out[l].astype(BF16), final_g, False)

        za, zb, zc = _inproj_call(x, sh2, sc2, _pad_inproj(w_in[l]).astype(BF16), tabs)

        pc, pd, pw, pg = _rwkv_prep_call(
            za, rwkv_mu[l].reshape(1, -1), rwkv_w0[l].reshape(1, 512), _block_diag2(rwkv_w_up[l]),
            rwkv_a0[l].reshape(1, 512), _block_diag2(rwkv_a_up[l]), rwkv_g_up[l],
            rwkv_k_k[l].reshape(1, 256), rwkv_k_a[l].reshape(1, 256), rwkv_r_k[l].reshape(1, 256), ones_bd)
        yf, yb = _rwkv_scan_call(pc, pd, pw)

        wk, wv = _split_wkv(mla_w_ukv[l])
        qt, k, vt = _mla_prep_call(zb, tabs, mla_q_norm_g[l].reshape(1, -1), _pad_wq(mla_w_uq[l]).astype(BF16),
                                   mla_kv_norm_g[l].reshape(1, -1), wk.astype(BF16), wv.T.astype(BF16))
        o_mla = _mla_attn_call(qt, k, vt)

        lr = ret_log_rate[l]
        lr_vec = jnp.repeat(lr, HEAD_DIM, axis=1)
        lr_heads = jnp.broadcast_to(lr.reshape(8, 1), (8, 128))
        rf, rb = _ret_call(zc, lr_vec, lr_heads)

        wo = w_out[l].astype(BF16)
        x = _outproj_call(x, g2, yf, yb, pg, rwkv_ln_g[l].reshape(1, 256), rwkv_ln_b[l].reshape(1, 256), o_mla,
                          rf, rb, zc, ret_gn_g[l].reshape(1, 256), avg_bd,
                          wo[0:256], wo[256:768], wo[768:1024])

        x = _ffn_call(x, sh3, sc3, g3, w_ff2_in[l].astype(BF16), w_ff2_out[l].astype(BF16), final_g, l == L - 1)
    return x
```

```python
import functools
import math

import jax
import jax.numpy as jnp
from jax import lax
from jax.experimental import pallas as pl
from jax.experimental.pallas import tpu as pltpu

F32 = jnp.float32
BF16 = jnp.bfloat16
HIGHEST = lax.Precision.HIGHEST

D_MODEL = 1024
HEAD_DIM = 64
RWKV_WIDTH = 256
RWKV_HEADS = 4
RWKV_LORA = 64
RWKV_GATE_LORA = 128
RWKV_COLS = 1152
MLA_HEADS = 8
MLA_NOPE = 64
MLA_ROPE = 32
MLA_V = 64
MLA_Q_RANK = 384
MLA_KV_RANK = 256
MLA_WIDTH = 512
MLA_HEAD_PAD = 128
MLA_TQ = 2048
MLA_QCOLS = 512
MLA_TK = 512
MLA_ONES_ROWS = 16
RET_WIDTH = 256
RET_HEADS = 4
RET_CHUNK = 128
RET_BATCH_PER_STEP = 4
RWKV_CHUNK = 64
RWKV_BATCH_PER_STEP = 4
RWKV_PASSES = 1
D_FF = 2816
FFN_TM = 512
FFN_TF = 256
N_MOD = 9
ROPE_BASE = 10000.0
NORM_EPS = 1e-6
RWKV_LN_EPS = 64e-5
RET_LN_EPS = 1e-5

VMEM_LIMIT = 48 * 1024 * 1024


def _cparams(*sem):
    return pltpu.CompilerParams(dimension_semantics=sem, vmem_limit_bytes=VMEM_LIMIT)


def _sigmoid(x):
    return 1.0 / (1.0 + jnp.exp(-x))


def _dot(a, b):
    return jnp.dot(a, b, preferred_element_type=F32)


def _dot_hi(a, b):
    return jnp.dot(a, b, precision=HIGHEST, preferred_element_type=F32)


def _split(a):
    hi = a.astype(BF16)
    lo = (a - hi.astype(F32)).astype(BF16)
    return hi, lo


def _head_sums(a, sel):
    ah, al = _split(a)
    sel = sel.astype(BF16)
    return _dot(ah, sel) + _dot(al, sel)


def _dg3(a, b, dims):
    ah, al = _split(a)
    bh, bl = _split(b)
    dn = (dims, ((), ()))
    out = lax.dot_general(ah, bh, dn, preferred_element_type=F32)
    out = out + lax.dot_general(ah, bl, dn, preferred_element_type=F32)
    out = out + lax.dot_general(al, bh, dn, preferred_element_type=F32)
    return out


_NN = ((1,), (0,))
_NT = ((1,), (1,))
_TN = ((0,), (0,))


def _rms_mod(x, sh, sc):
    ms = jnp.mean(x * x, axis=-1, keepdims=True)
    return (x * lax.rsqrt(ms + NORM_EPS)) * (1.0 + sc) + sh


def _mod_kernel(c_ref, w_ref, b_ref, o_ref):
    c = c_ref[...]
    cond = c * _sigmoid(c)
    o_ref[0] = _dot_hi(cond, w_ref[0]) + b_ref[0]


def _mod_call(c, w_ada, b_ada):
    L, D, N = w_ada.shape
    B = c.shape[0]
    tn = 1152
    return pl.pallas_call(
        _mod_kernel,
        grid=(L, N // tn),
        in_specs=[
            pl.BlockSpec((B, D), lambda l, j: (0, 0)),
            pl.BlockSpec((1, D, tn), lambda l, j: (l, 0, j)),
            pl.BlockSpec((1, 1, tn), lambda l, j: (l, 0, j)),
        ],
        out_specs=pl.BlockSpec((1, B, tn), lambda l, j: (l, 0, j)),
        out_shape=jax.ShapeDtypeStruct((L, B, N), F32),
        compiler_params=_cparams("parallel", "parallel"),
        name="adaln_mod",
    )(c, w_ada, b_ada.reshape(L, 1, N))


def _rope_kernel(pos_ref, inv_ref, mc_ref, m1_ref, ma_ref, mb_ref, o_ref, ot_ref):
    pos = pos_ref[0].astype(F32)
    for i in range(2):
        ang = pos * inv_ref[i:i + 1, :]
        cos = jnp.cos(ang)
        sin = jnp.sin(ang)
        o_ref[i, 0, :, 0:128] = cos * mc_ref[i:i + 1, :] + m1_ref[i:i + 1, :]
        o_ref[i, 0, :, 128:256] = sin * ma_ref[i:i + 1, :]
        o_ref[i, 0, :, 256:384] = sin * mb_ref[i:i + 1, :]
    ot_ref[0] = o_ref[0, 0].T


def _rope_consts():
    lane = jnp.arange(128)
    inv_m = ROPE_BASE ** (-jnp.arange(0, MLA_ROPE, 2, dtype=F32) / MLA_ROPE)
    in_rope = (lane >= 64) & (lane < 96)
    inv0 = jnp.where(in_rope, inv_m[(lane - 64) % 16], 0.0)
    mc0 = in_rope.astype(F32)
    m10 = (lane < 64).astype(F32)
    ma0 = jnp.where((lane >= 64) & (lane < 80), -1.0, 0.0)
    mb0 = jnp.where((lane >= 80) & (lane < 96), 1.0, 0.0)
    inv_r = ROPE_BASE ** (-jnp.arange(0, HEAD_DIM, 2, dtype=F32) / HEAD_DIM)
    inv1 = inv_r[lane % 32]
    mc1 = jnp.ones((128,), F32)
    m11 = jnp.zeros((128,), F32)
    ma1 = jnp.where((lane % 64) < 32, -1.0, 0.0)
    mb1 = jnp.where((lane % 64) >= 32, 1.0, 0.0)
    st = lambda a, b: jnp.stack([a, b]).astype(F32)
    return st(inv0, inv1), st(mc0, mc1), st(m10, m11), st(ma0, ma1), st(mb0, mb1)


def _rope_call(positions):
    B, S = positions.shape
    ts = min(S, 512)
    consts = _rope_consts()
    cspec = pl.BlockSpec((2, 128), lambda b, i: (0, 0))
    return pl.pallas_call(
        _rope_kernel,
        grid=(B, S // ts),
        in_specs=[pl.BlockSpec((1, ts, 1), lambda b, i: (b, i, 0))] + [cspec] * 5,
        out_specs=[pl.BlockSpec((2, 1, ts, 384), lambda b, i: (0, b, i, 0)),
                   pl.BlockSpec((1, 384, ts), lambda b, i: (b, 0, i))],
        out_shape=[jax.ShapeDtypeStruct((2, B, S, 384), F32), jax.ShapeDtypeStruct((B, 384, S), F32)],
        compiler_params=_cparams("parallel", "parallel"),
        name="rope_tables",
    )(positions.reshape(B, S, 1), *consts)


def _apply_rope(x, tab, shift):
    n = x.shape[1] // 128
    width = x.shape[1]
    cc = jnp.tile(tab[:, 0:128], (1, n))
    sa = jnp.tile(tab[:, 128:256], (1, n))
    sb = jnp.tile(tab[:, 256:384], (1, n))
    return x * cc + pltpu.roll(x, width - shift, axis=1) * sa + pltpu.roll(x, shift, axis=1) * sb


def _ffn_kernel(x_ref, sh_ref, sc_ref, g_ref, wi_ref, wo_ref, fg_ref, o_ref, *, final):
    x = x_ref[0]
    h = _rms_mod(x, sh_ref[0], sc_ref[0]).astype(BF16)
    acc = None
    for c in range(D_FF // FFN_TF):
        cols = slice(c * FFN_TF, (c + 1) * FFN_TF)
        gate = _dot(h, wi_ref[:, cols])
        up = _dot(h, wi_ref[:, D_FF + c * FFN_TF:D_FF + (c + 1) * FFN_TF])
        act = (gate * _sigmoid(gate) * up).astype(BF16)
        part = _dot(act, wo_ref[cols, :])
        acc = part if acc is None else acc + part
    y = x + (0.5 * g_ref[0]) * acc
    if final:
        ms = jnp.mean(y * y, axis=-1, keepdims=True)
        y = (y * lax.rsqrt(ms + NORM_EPS)) * fg_ref[...]
    o_ref[0] = y


def _ffn_call(x, sh, sc, g, w_in_bf, w_out_bf, final_g, final):
    B, S, D = x.shape
    tm = min(S, FFN_TM)
    vec = pl.BlockSpec((1, 1, D), lambda b, i: (b, 0, 0))
    resident = lambda shape: pl.BlockSpec(shape, lambda b, i: (0, 0), pipeline_mode=pl.Buffered(1))
    return pl.pallas_call(
        functools.partial(_ffn_kernel, final=final),
        grid=(B, S // tm),
        in_specs=[
            pl.BlockSpec((1, tm, D), lambda b, i: (b, i, 0)),
            vec, vec, vec,
            resident((D, 2 * D_FF)), resident((D_FF, D)),
            pl.BlockSpec((1, D), lambda b, i: (0, 0)),
        ],
        out_specs=pl.BlockSpec((1, tm, D), lambda b, i: (b, i, 0)),
        out_shape=jax.ShapeDtypeStruct((B, S, D), F32),
        compiler_params=_cparams("parallel", "parallel"),
        name="ffn",
    )(x, sh, sc, g, w_in_bf, w_out_bf, final_g)


ZB_COLS = MLA_Q_RANK + MLA_KV_RANK + MLA_HEAD_PAD
ZC_COLS = 4 * RET_WIDTH
Z_COLS = RWKV_COLS + ZB_COLS + ZC_COLS


def _inproj_kernel(x_ref, sh_ref, sc_ref, w_ref, tab_ref, za_ref, zb_ref, zc_ref):
    h = _rms_mod(x_ref[0], sh_ref[0], sc_ref[0]).astype(BF16)
    z = _dot(h, w_ref[...])
    za_ref[0] = z[:, 0:RWKV_COLS]
    zb_ref[0] = z[:, RWKV_COLS:RWKV_COLS + ZB_COLS].astype(BF16)
    o = RWKV_COLS + ZB_COLS
    tab = tab_ref[0, 0]
    zc_ref[0, :, 0:256] = _apply_rope(z[:, o:o + 256], tab, 32).astype(BF16)
    zc_ref[0, :, 256:512] = (_apply_rope(z[:, o + 256:o + 512], tab, 32) * (HEAD_DIM ** -0.5)).astype(BF16)
    zc_ref[0, :, 512:1024] = z[:, o + 512:Z_COLS].astype(BF16)


def _inproj_call(x, sh, sc, wz_bf, tabs):
    B, S, D = x.shape
    tm = min(S, 512)
    vec = pl.BlockSpec((1, 1, D), lambda b, i: (b, 0, 0))
    out = lambda n: pl.BlockSpec((1, tm, n), lambda b, i: (b, i, 0))
    return pl.pallas_call(
        _inproj_kernel,
        grid=(B, S // tm),
        in_specs=[
            pl.BlockSpec((1, tm, D), lambda b, i: (b, i, 0)),
            vec, vec,
            pl.BlockSpec((D, Z_COLS), lambda b, i: (0, 0)),
            pl.BlockSpec((1, 1, tm, 384), lambda b, i: (1, b, i, 0)),
        ],
        out_specs=[out(RWKV_COLS), out(ZB_COLS), out(ZC_COLS)],
        out_shape=[jax.ShapeDtypeStruct((B, S, n), dt)
                   for n, dt in ((RWKV_COLS, F32), (ZB_COLS, BF16), (ZC_COLS, BF16))],
        compiler_params=_cparams("parallel", "parallel"),
        name="inproj",
    )(x, sh, sc, wz_bf, tabs)


def _rwkv_prep_kernel(z_ref, zp_ref, zn_ref, mu_ref, w0_ref, wup_ref, a0_ref, aup_ref, gup_ref,
                      kk_ref, ka_ref, rk_ref, ones_ref, pc_ref, pd_ref, pw_ref, pg_ref, *, ts, nt):
    i = pl.program_id(1)
    z = z_ref[0]
    row = lax.broadcasted_iota(jnp.int32, (ts, 1), 0)
    prev_edge = jnp.where(i > 0, zp_ref[0, 7:8, :], 0.0)
    next_edge = jnp.where(i < nt - 1, zn_ref[0, 0:1, :], 0.0)
    prev = jnp.where(row == 0, prev_edge, pltpu.roll(z, 1, axis=0))
    nxt = jnp.where(row == ts - 1, next_edge, pltpu.roll(z, ts - 1, axis=0))
    zs = z + mu_ref[...] * (0.5 * (prev + nxt) - z)

    r = zs[:, 0:256]
    k = zs[:, 256:512]
    v = zs[:, 512:768]
    g_lo = zs[:, 768:896]
    w_lo = zs[:, 896:1024]
    a_lo = zs[:, 1024:1152]

    w_raw = w0_ref[...] + _dg3(jnp.tanh(w_lo), wup_ref[...], _NN)
    logw = (-math.exp(-0.5)) * _sigmoid(w_raw)
    a = _sigmoid(a0_ref[...] + _dg3(a_lo, aup_ref[...], _NN))
    g = _dg3(_sigmoid(g_lo), gup_ref[...], _NN)

    ones_bd = ones_ref[...]
    kk0 = k * kk_ref[...]
    nrm = jnp.sqrt(_head_sums(kk0 * kk0, ones_bd))
    kk = kk0 / jnp.maximum(nrm, 1e-12)
    bonus = _head_sums(r * k * rk_ref[...], ones_bd) * v

    pc_ref[0, :, 0:256] = r.astype(BF16)
    pc_ref[0, :, 256:512] = v.astype(BF16)
    pc_ref[0, :, 512:768] = kk.astype(BF16)
    for d in range(2):
        a_d = a[:, 256 * d:256 * d + 256]
        pd_ref[d, 0, :, 0:256] = (k * (1.0 + (a_d - 1.0) * ka_ref[...])).astype(BF16)
        pd_ref[d, 0, :, 256:512] = (kk * a_d).astype(BF16)
        pw_ref[d, 0] = logw[:, 256 * d:256 * d + 256]
    pg_ref[0, :, 0:256] = g.astype(BF16)
    pg_ref[0, :, 256:512] = bonus.astype(BF16)


def _rwkv_prep_call(za, mu, w0, wup_bd, a0, aup_bd, gup, k_k, k_a, r_k, ones_bd):
    B, S, _ = za.shape
    ts = min(S, 512)
    nt = S // ts
    nb8 = S // 8
    const = lambda shape: pl.BlockSpec(shape, lambda b, i: (0,) * len(shape))
    return pl.pallas_call(
        functools.partial(_rwkv_prep_kernel, ts=ts, nt=nt),
        grid=(B, nt),
        in_specs=[
            pl.BlockSpec((1, ts, RWKV_COLS), lambda b, i: (b, i, 0)),
            pl.BlockSpec((1, 8, RWKV_COLS), lambda b, i: (b, jnp.maximum(i * (ts // 8) - 1, 0), 0)),
            pl.BlockSpec((1, 8, RWKV_COLS), lambda b, i: (b, jnp.minimum((i + 1) * (ts // 8), nb8 - 1), 0)),
            const((1, RWKV_COLS)),
            const((1, 512)), const((128, 512)), const((1, 512)), const((128, 512)), const((128, 256)),
            const((1, 256)), const((1, 256)), const((1, 256)), const((256, 256)),
        ],
        out_specs=[
            pl.BlockSpec((1, ts, 768), lambda b, i: (b, i, 0)),
            pl.BlockSpec((2, 1, ts, 512), lambda b, i: (0, b, i, 0)),
            pl.BlockSpec((2, 1, ts, 256), lambda b, i: (0, b, i, 0)),
            pl.BlockSpec((1, ts, 512), lambda b, i: (b, i, 0)),
        ],
        out_shape=[
            jax.ShapeDtypeStruct((B, S, 768), BF16),
            jax.ShapeDtypeStruct((2, B, S, 512), BF16),
            jax.ShapeDtypeStruct((2, B, S, 256), F32),
            jax.ShapeDtypeStruct((B, S, 512), BF16),
        ],
        compiler_params=_cparams("parallel", "parallel"),
        name="rwkv_prep",
    )(za, za, za, mu, w0, wup_bd, a0, aup_bd, gup, k_k, k_a, r_k, ones_bd)


def _rwkv_dir_operands(direction, pc, pd, logw):
    C = RWKV_CHUNK
    r = pc[:, 0:256]
    v = pc[:, 256:512]
    kk = pc[:, 512:768]
    kmod = pd[:, 0:256]
    kka = pd[:, 256:512]

    row = lax.broadcasted_iota(jnp.int32, (C, C), 0)
    col = lax.broadcasted_iota(jnp.int32, (C, C), 1)
    tri = jnp.where((col <= row) if direction == 0 else (col >= row), 1.0, 0.0).astype(BF16)
    lw_hi, lw_lo = _split(logw)
    l_in = _dot(tri, lw_hi) + _dot(tri, lw_lo)
    l_ex = l_in - logw
    l_tot = l_in[C - 1:C, :] if direction == 0 else l_in[0:1, :]
    e_ex = jnp.exp(l_ex)
    e_neg = jnp.exp(-l_in)
    e_tot = jnp.exp(l_tot - l_in)
    kkg = kk * e_ex
    bbar = kka * e_neg
    kbar = kmod * e_neg
    bg = kka * e_tot
    kg = kmod * e_tot
    rg = r * jnp.exp(l_in) if direction == 0 else r * e_ex
    g_tot = jnp.exp(l_tot)
    return dict(kkg=kkg, bbar=bbar, kbar=kbar, bg=bg, kg=kg, rg=rg, v=v, g_tot=g_tot)


def _operand(a, passes):
    hi = a.astype(BF16)
    lo = (a - hi.astype(F32)).astype(BF16) if passes > 1 else None
    return hi, lo


def _mm(a, b, dims):
    dn = (dims, ((), ()))
    out = lax.dot_general(a[0], b[0], dn, preferred_element_type=F32)
    if b[1] is not None:
        out = out + lax.dot_general(a[0], b[1], dn, preferred_element_type=F32)
    if a[1] is not None:
        out = out + lax.dot_general(a[1], b[0], dn, preferred_element_type=F32)
    return out


def _rwkv_scan_kernel(pcf_ref, pcb_ref, pdf_ref, pdb_ref, pwf_ref, pwb_ref, yf_ref, yb_ref, s_ref):
    C = RWKV_CHUNK
    P = RWKV_PASSES

    @pl.when(pl.program_id(1) == 0)
    def _():
        s_ref[...] = jnp.zeros_like(s_ref)

    row = lax.broadcasted_iota(jnp.int32, (C, C), 0)
    col = lax.broadcasted_iota(jnp.int32, (C, C), 1)
    eye = col == row
    ident = jnp.where(eye, 1.0, 0.0)
    nb = pcf_ref.shape[0]
    ops = {}
    for bb in range(nb):
        ops[bb, 0] = _rwkv_dir_operands(0, pcf_ref[bb].astype(F32), pdf_ref[0, bb].astype(F32), pwf_ref[0, bb])
        ops[bb, 1] = _rwkv_dir_operands(1, pcb_ref[bb].astype(F32), pdb_ref[0, bb].astype(F32), pwb_ref[0, bb])
    earlier = (col < row, col > row)
    ymask = (col <= row, col > row)
    units = [(bb, d, h) for bb in range(nb) for d in range(2) for h in range(RWKV_HEADS)]
    nu = len(units)
    sl = lambda h: slice(64 * h, 64 * h + 64)
    s0 = [s_ref[u] for u in units]

    def head(name):
        return [_operand(ops[bb, d][name][:, sl(h)], P) for bb, d, h in units]

    kkg, bbar, kbar, bg, kg, rg, v = (head(n) for n in ("kkg", "bbar", "kbar", "bg", "kg", "rg", "v"))
    akk = [jnp.where(earlier[d], _mm(kkg[i], bbar[i], _NT), 0.0) for i, (_, d, _) in enumerate(units)]
    bk = [jnp.where(earlier[d], _mm(kkg[i], kbar[i], _NT), 0.0) for i, (_, d, _) in enumerate(units)]
    ark = [jnp.where(ymask[d], _mm(rg[i], bbar[i], _NT), 0.0) for i, (_, d, _) in enumerate(units)]
    brk = [jnp.where(ymask[d], _mm(rg[i], kbar[i], _NT), 0.0) for i, (_, d, _) in enumerate(units)]
    pw = [-a for a in akk]
    t_inv = [ident + x for x in pw]
    for _ in range(int(math.log2(C)) - 1):
        pw_o = [_operand(x, P) for x in pw]
        pw = [_mm(x, x, _NN) for x in pw_o]
        pw_o = [_operand(x, P) for x in pw]
        t_inv = [t + _mm(_operand(t, P), x, _NN) for t, x in zip(t_inv, pw_o)]
    t_o = [_operand(t, P) for t in t_inv]
    bkv = [_mm(_operand(b, P), x, _NN) for b, x in zip(bk, v)]
    w_m = [_mm(t, x, _NN) for t, x in zip(t_o, kkg)]
    u_m = [_mm(t, _operand(x, P), _NN) for t, x in zip(t_o, bkv)]
    w_o = [_operand(x, P) for x in w_m]
    u_o = [_operand(x, P) for x in u_m]
    ark_o = [_operand(x, P) for x in ark]
    brk_o = [_operand(x, P) for x in brk]
    m_m = [jnp.where(eye, ops[bb, d]["g_tot"][:, sl(h)], 0.0) - _mm(w_o[i], bg[i], _TN)
           for i, (bb, d, h) in enumerate(units)]
    n_m = [_mm(v[i], kg[i], _TN) - _mm(u_o[i], bg[i], _TN) for i in range(nu)]
    q_m = [ops[bb, d]["rg"][:, sl(h)] - _mm(ark_o[i], w_o[i], _NN) for i, (bb, d, h) in enumerate(units)]
    y_i = [_mm(brk_o[i], v[i], _NN) - _mm(ark_o[i], u_o[i], _NN) for i in range(nu)]
    s_o = [_operand(s, P) for s in s0]
    y = [y_i[i] + _mm(_operand(q_m[i], P), s_o[i], _NT) for i in range(nu)]
    s_new = [_mm(s_o[i], _operand(m_m[i], P), _NN) + n_m[i] for i in range(nu)]
    for i, (bb, d, h) in enumerate(units):
        (yf_ref if d == 0 else yb_ref)[bb, :, sl(h)] = y[i]
        s_ref[bb, d, h] = s_new[i]


def _rwkv_scan_call(pc, pd, pw):
    B, S, _ = pc.shape
    C = RWKV_CHUNK
    nc = S // C
    nb = math.gcd(B, RWKV_BATCH_PER_STEP)
    return pl.pallas_call(
        _rwkv_scan_kernel,
        grid=(B // nb, nc),
        in_specs=[
            pl.BlockSpec((nb, C, 768), lambda b, c: (b, c, 0)),
            pl.BlockSpec((nb, C, 768), lambda b, c: (b, nc - 1 - c, 0)),
            pl.BlockSpec((1, nb, C, 512), lambda b, c: (0, b, c, 0)),
            pl.BlockSpec((1, nb, C, 512), lambda b, c: (1, b, nc - 1 - c, 0)),
            pl.BlockSpec((1, nb, C, 256), lambda b, c: (0, b, c, 0)),
            pl.BlockSpec((1, nb, C, 256), lambda b, c: (1, b, nc - 1 - c, 0)),
        ],
        out_specs=[
            pl.BlockSpec((nb, C, 256), lambda b, c: (b, c, 0)),
            pl.BlockSpec((nb, C, 256), lambda b, c: (b, nc - 1 - c, 0)),
        ],
        out_shape=[jax.ShapeDtypeStruct((B, S, 256), F32)] * 2,
        scratch_shapes=[pltpu.VMEM((nb, 2, RWKV_HEADS, HEAD_DIM, HEAD_DIM), F32)],
        compiler_params=_cparams("parallel", "arbitrary"),
        name="rwkv_scan",
    )(pc, pc, pd, pd, pw, pw)


def _mla_prep_kernel(zb_ref, tab_ref, tabt_ref, qg_ref, wqt_ref, kg_ref, wk_ref, wvt_ref, qt_ref, k_ref, vt_ref):
    zb = zb_ref[0].astype(F32)
    tab = tab_ref[0, 0]
    cq = zb[:, 0:MLA_Q_RANK]
    ckv = zb[:, MLA_Q_RANK:MLA_Q_RANK + MLA_KV_RANK]
    kpe = zb[:, MLA_Q_RANK + MLA_KV_RANK:ZB_COLS]

    def rms(t):
        return t * lax.rsqrt(jnp.mean(t * t, axis=-1, keepdims=True) + NORM_EPS)

    cqn = (rms(cq) * qg_ref[...]).astype(BF16)
    ckvn = (rms(ckv) * kg_ref[...]).astype(BF16)
    nt_dims = (_NT, ((), ()))
    qt = lax.dot_general(wqt_ref[...], cqn, nt_dims, preferred_element_type=F32)
    rows = qt.shape[0]
    tabt = tabt_ref[0]
    cc, sa, sb = (jnp.tile(tabt[128 * j:128 * j + 128], (MLA_HEADS, 1)) for j in range(3))
    qt = qt * cc + pltpu.roll(qt, rows - 16, axis=0) * sa + pltpu.roll(qt, 16, axis=0) * sb
    k = _dot(ckvn, wk_ref[...])
    kpe = _apply_rope(kpe, tab, 16)
    k = k + jnp.tile(kpe, (1, MLA_HEADS))
    qt_ref[0] = (qt * (math.log2(math.e) * (MLA_NOPE + MLA_ROPE) ** -0.5)).astype(BF16)
    k_ref[0] = k.astype(BF16)
    vt_ref[0, 0] = lax.dot_general(wvt_ref[...], ckvn, nt_dims, preferred_element_type=F32).astype(BF16)


def _mla_key_tile(S):
    return min(S, MLA_TK)


def _mla_prep_call(zb, tabs, tab_t, qg, wqt_bf, kg, wk_bf, wvt_bf):
    B, S, _ = zb.shape
    ts = _mla_key_tile(S)
    HP = MLA_HEADS * MLA_HEAD_PAD
    const = lambda shape: pl.BlockSpec(shape, lambda b, i: (0,) * len(shape))
    return pl.pallas_call(
        _mla_prep_kernel,
        grid=(B, S // ts),
        in_specs=[
            pl.BlockSpec((1, ts, ZB_COLS), lambda b, i: (b, i, 0)),
            pl.BlockSpec((1, 1, ts, 384), lambda b, i: (0, b, i, 0)),
            pl.BlockSpec((1, 384, ts), lambda b, i: (b, 0, i)),
            const((1, MLA_Q_RANK)), const((HP, MLA_Q_RANK)),
            const((1, MLA_KV_RANK)), const((MLA_KV_RANK, HP)), const((MLA_WIDTH, MLA_KV_RANK)),
        ],
        out_specs=[
            pl.BlockSpec((1, HP, ts), lambda b, i: (b, 0, i)),
            pl.BlockSpec((1, ts, HP), lambda b, i: (b, i, 0)),
            pl.BlockSpec((1, 1, MLA_WIDTH, ts), lambda b, i: (b, i, 0, 0)),
        ],
        out_shape=[
            jax.ShapeDtypeStruct((B, HP, S), BF16),
            jax.ShapeDtypeStruct((B, S, HP), BF16),
            jax.ShapeDtypeStruct((B, S // ts, MLA_WIDTH, ts), BF16),
        ],
        compiler_params=_cparams("parallel", "parallel"),
        name="mla_prep",
    )(zb, tabs, tab_t, qg, wqt_bf, kg, wk_bf, wvt_bf)


def _mla_attn_kernel(qt_ref, k_ref, vt_ref, o_ref, *, tk, nk):
    tq = qt_ref.shape[2]
    cw = min(tq, MLA_QCOLS)
    ncol = tq // cw
    chunks = [(hh, c) for hh in range(2) for c in range(ncol)]
    ones = jnp.ones((MLA_ONES_ROWS, tk), BF16)

    def body(i, carry):
        ks = pl.multiple_of(i * tk, tk)
        scores = [_dot(k_ref[0, pl.ds(ks, tk), 128 * hh:128 * hh + 128],
                       qt_ref[0, 128 * hh:128 * hh + 128, c * cw:(c + 1) * cw]) for hh, c in chunks]
        stats = []
        for j, s in enumerate(scores):
            m = carry[j][0]
            m_new = jnp.maximum(m, jnp.max(s, axis=0, keepdims=True))
            stats.append((m_new, jnp.exp2(m - m_new), jnp.exp2(s - m_new).astype(BF16)))
        new = []
        for j, (hh, c) in enumerate(chunks):
            m_new, alpha, p = stats[j]
            vt_ones = jnp.concatenate([vt_ref[0, i, 64 * hh:64 * hh + 64, :], ones], axis=0)
            new.append((m_new, alpha * carry[j][1] + _dot(vt_ones, p)))
        return tuple(new)

    init = (jnp.full((1, cw), -jnp.inf, F32), jnp.zeros((MLA_V + MLA_ONES_ROWS, cw), F32))
    res = lax.fori_loop(0, nk, body, (init,) * len(chunks))
    out_t = [jnp.concatenate([res[hh * ncol + c][1][0:MLA_V] / res[hh * ncol + c][1][MLA_V:MLA_V + 1]
                              for c in range(ncol)], axis=1) for hh in range(2)]
    o_ref[0] = jnp.concatenate(out_t, axis=0).T.astype(BF16)


def _mla_attn_call(qt, k, vt):
    B, S, _ = k.shape
    tq = min(S, MLA_TQ)
    tk = _mla_key_tile(S)
    nk = S // tk
    return pl.pallas_call(
        functools.partial(_mla_attn_kernel, tk=tk, nk=nk),
        grid=(B, MLA_HEADS // 2, S // tq),
        in_specs=[
            pl.BlockSpec((1, 256, tq), lambda b, h, i: (b, h, i)),
            pl.BlockSpec((1, S, 256), lambda b, h, i: (b, 0, h)),
            pl.BlockSpec((1, nk, 128, tk), lambda b, h, i: (b, 0, h, 0)),
        ],
        out_specs=pl.BlockSpec((1, tq, 128), lambda b, h, i: (b, i, h)),
        out_shape=jax.ShapeDtypeStruct((B, S, MLA_WIDTH), BF16),
        compiler_params=_cparams("parallel", "parallel", "arbitrary"),
        name="mla_attn",
    )(qt, k, vt)


def _ret_dir_operands(direction, zc, lrv, lrh):
    C = RET_CHUNK
    q = zc[:, 0:256]
    k = zc[:, 256:512]
    v = zc[:, 512:768]
    lgv = -jnp.exp(lrv)
    lgh = -jnp.exp(lrh)
    pos = lax.broadcasted_iota(jnp.int32, (C, 1), 0).astype(F32)
    row = lax.broadcasted_iota(jnp.int32, (C, C), 0)
    col = lax.broadcasted_iota(jnp.int32, (C, C), 1)
    if direction == 0:
        kw = k * jnp.exp(lgv * (C - 1.0 - pos))
        qw = q * jnp.exp(lgv * (pos + 1.0))
        mask = col <= row
        dist = (row - col).astype(F32)
    else:
        kw = k * jnp.exp(lgv * pos)
        qw = q * jnp.exp(lgv * (C - pos))
        mask = col > row
        dist = (col - row).astype(F32)
    dist = jnp.maximum(dist, 0.0)
    dmats = [jnp.where(mask, jnp.exp(lgh[h:h + 1, :] * dist), 0.0) for h in range(RET_HEADS)]
    decays = [jnp.exp(lgh[h:h + 1, 0:64] * C) for h in range(RET_HEADS)]
    qb, kb, vb, kwb, qwb = (t.astype(BF16) for t in (q, k, v, kw, qw))
    return dict(q=qb, k=kb, v=vb, kw=kwb, qw=qwb, dmat=dmats, decay=decays)


def _ret_kernel(zf_ref, zb_ref, lrv_ref, lrh_ref, yf_ref, yb_ref, r_ref):
    @pl.when(pl.program_id(1) == 0)
    def _():
        r_ref[...] = jnp.zeros_like(r_ref)

    nb = zf_ref.shape[0]
    ops = {}
    for bb in range(nb):
        ops[bb, 0] = _ret_dir_operands(0, zf_ref[bb].astype(F32), lrv_ref[0:1, :], lrh_ref[0:4, :])
        ops[bb, 1] = _ret_dir_operands(1, zb_ref[bb].astype(F32), lrv_ref[1:2, :], lrh_ref[4:8, :])
    units = [(bb, d, h) for bb in range(nb) for d in range(2) for h in range(RET_HEADS)]
    sl = lambda h: slice(64 * h, 64 * h + 64)
    r0 = [r_ref[u] for u in units]
    sc = [lax.dot_general(ops[bb, d]["q"][:, sl(h)], ops[bb, d]["k"][:, sl(h)], (_NT, ((), ())),
                          preferred_element_type=F32) * ops[bb, d]["dmat"][h] for bb, d, h in units]
    kv = [lax.dot_general(ops[bb, d]["kw"][:, sl(h)], ops[bb, d]["v"][:, sl(h)], (_TN, ((), ())),
                          preferred_element_type=F32) for bb, d, h in units]
    cross = [_dot(ops[bb, d]["qw"][:, sl(h)], r0[i].astype(BF16)) for i, (bb, d, h) in enumerate(units)]
    inner = [_dot(sc[i].astype(BF16), ops[bb, d]["v"][:, sl(h)]) for i, (bb, d, h) in enumerate(units)]
    for i, (bb, d, h) in enumerate(units):
        (yf_ref if d == 0 else yb_ref)[bb, :, sl(h)] = inner[i] + cross[i]
        r_ref[bb, d, h] = r0[i] * ops[bb, d]["decay"][h] + kv[i]


def _ret_call(zc, lr_vec, lr_heads):
    B, S, _ = zc.shape
    C = RET_CHUNK
    nc = S // C
    nb = math.gcd(B, RET_BATCH_PER_STEP)
    const = lambda shape: pl.BlockSpec(shape, lambda b, c: (0,) * len(shape))
    return pl.pallas_call(
        _ret_kernel,
        grid=(B // nb, nc),
        in_specs=[
            pl.BlockSpec((nb, C, ZC_COLS), lambda b, c: (b, c, 0)),
            pl.BlockSpec((nb, C, ZC_COLS), lambda b, c: (b, nc - 1 - c, 0)),
            const((2, 256)), const((8, 128)),
        ],
        out_specs=[
            pl.BlockSpec((nb, C, 256), lambda b, c: (b, c, 0)),
            pl.BlockSpec((nb, C, 256), lambda b, c: (b, nc - 1 - c, 0)),
        ],
        out_shape=[jax.ShapeDtypeStruct((B, S, 256), F32)] * 2,
        scratch_shapes=[pltpu.VMEM((nb, 2, RET_HEADS, HEAD_DIM, HEAD_DIM), F32)],
        compiler_params=_cparams("parallel", "arbitrary"),
        name="retention",
    )(zc, zc, lr_vec, lr_heads)


def _head_norm(y, avg_bd, eps):
    mu = _head_sums(y, avg_bd)
    d = y - mu
    var = _head_sums(d * d, avg_bd)
    return d * lax.rsqrt(var + eps)


def _outproj_kernel(x_ref, g2_ref, yf_ref, yb_ref, pg_ref, lng_ref, lnb_ref, om_ref, rf_ref, rb_ref,
                    gate_ref, gng_ref, avg_ref, wa_ref, wb_ref, wc_ref, o_ref):
    avg_bd = avg_ref[...]
    y = _head_norm(yf_ref[0] + yb_ref[0], avg_bd, RWKV_LN_EPS) * lng_ref[...] + lnb_ref[...]
    pg = pg_ref[0].astype(F32)
    o_a = (y + pg[:, 256:512]) * pg[:, 0:256]
    yr = _head_norm(rf_ref[0] + rb_ref[0], avg_bd, RET_LN_EPS) * gng_ref[...]
    gate = gate_ref[0].astype(F32)
    o_c = (gate * _sigmoid(gate)) * yr
    mixed = _dot(o_a.astype(BF16), wa_ref[...])
    mixed = mixed + _dot(om_ref[0], wb_ref[...])
    mixed = mixed + _dot(o_c.astype(BF16), wc_ref[...])
    o_ref[0] = x_ref[0] + g2_ref[0] * mixed


def _outproj_call(x, g2, yf, yb, pg, ln_g, ln_b, o_mla, rf, rb, zc, gn_g, avg_bd, wa, wb, wc):
    B, S, D = x.shape
    tm = min(S, 512)
    tok = lambda n: pl.BlockSpec((1, tm, n), lambda b, i: (b, i, 0))
    const = lambda shape: pl.BlockSpec(shape, lambda b, i: (0,) * len(shape))
    return pl.pallas_call(
        _outproj_kernel,
        grid=(B, S // tm),
        in_specs=[
            tok(D),
            pl.BlockSpec((1, 1, D), lambda b, i: (b, 0, 0)),
            tok(256), tok(256), tok(512), const((1, 256)), const((1, 256)),
            tok(512), tok(256), tok(256),
            pl.BlockSpec((1, tm, 256), lambda b, i: (b, i, 3)),
            const((1, 256)), const((256, 256)),
            const((256, D)), const((512, D)), const((256, D)),
        ],
        out_specs=tok(D),
        out_shape=jax.ShapeDtypeStruct((B, S, D), F32),
        compiler_params=_cparams("parallel", "parallel"),
        name="outproj",
    )(x, g2, yf, yb, pg, ln_g, ln_b, o_mla, rf, rb, zc, gn_g, avg_bd, wa, wb, wc)


def _block_diag2(w):
    z = jnp.zeros_like(w[0])
    return jnp.concatenate([jnp.concatenate([w[0], z], axis=1), jnp.concatenate([z, w[1]], axis=1)], axis=0)


def _pad_inproj(w):
    D = w.shape[0]
    o = RWKV_COLS + MLA_Q_RANK + MLA_KV_RANK
    return jnp.concatenate(
        [w[:, :o], jnp.zeros((D, 64), w.dtype), w[:, o:o + MLA_ROPE], jnp.zeros((D, 32), w.dtype), w[:, o + MLA_ROPE:]],
        axis=1)


def _pad_wq(w):
    r = w.shape[0]
    w = w.reshape(r, MLA_HEADS, MLA_NOPE + MLA_ROPE)
    w = jnp.concatenate([w, jnp.zeros((r, MLA_HEADS, 32), w.dtype)], axis=2)
    return w.reshape(r, MLA_HEADS * MLA_HEAD_PAD)


def _split_wkv(w):
    r = w.shape[0]
    w = w.reshape(r, MLA_HEADS, MLA_NOPE + MLA_V)
    wk = jnp.concatenate([w[:, :, :MLA_NOPE], jnp.zeros((r, MLA_HEADS, 64), w.dtype)], axis=2)
    return wk.reshape(r, MLA_HEADS * MLA_HEAD_PAD), w[:, :, MLA_NOPE:].reshape(r, MLA_WIDTH)


def kernel(x, c, positions, w_ada, b_ada, w_ff1_in, w_ff1_out, w_ff2_in, w_ff2_out, w_in, w_out, rwkv_mu, rwkv_w0,
           rwkv_w_up, rwkv_a0, rwkv_a_up, rwkv_g_up, rwkv_k_k, rwkv_k_a, rwkv_r_k, rwkv_ln_g, rwkv_ln_b,
           mla_q_norm_g, mla_w_uq, mla_kv_norm_g, mla_w_ukv, ret_log_rate, ret_gn_g, final_norm_g):
    B, S, D = x.shape
    L = w_ada.shape[0]
    mod = _mod_call(c, w_ada, b_ada)
    tabs, tab_t = _rope_call(positions)

    head_id = jnp.arange(256) // HEAD_DIM
    same_head = (head_id[:, None] == head_id[None, :]).astype(F32)
    ones_bd = same_head
    avg_bd = same_head / HEAD_DIM
    final_g = final_norm_g.reshape(1, D)

    for l in range(L):
        m = [mod[l, :, i * D:(i + 1) * D].reshape(B, 1, D) for i in range(N_MOD)]
        sh1, sc1, g1, sh2, sc2, g2, sh3, sc3, g3 = m

        x = _ffn_call(x, sh1, sc1, g1, w_ff1_in[l].astype(BF16), w_ff1_out[l].astype(BF16), final_g, False)

        za, zb, zc = _inproj_call(x, sh2, sc2, _pad_inproj(w_in[l]).astype(BF16), tabs)

        pc, pd, pw, pg = _rwkv_prep_call(
            za, rwkv_mu[l].reshape(1, -1), rwkv_w0[l].reshape(1, 512), _block_diag2(rwkv_w_up[l]),
            rwkv_a0[l].reshape(1, 512), _block_diag2(rwkv_a_up[l]), rwkv_g_up[l],
            rwkv_k_k[l].reshape(1, 256), rwkv_k_a[l].reshape(1, 256), rwkv_r_k[l].reshape(1, 256), ones_bd)
        yf, yb = _rwkv_scan_call(pc, pd, pw)

        wk, wv = _split_wkv(mla_w_ukv[l])
        qt, k, vt = _mla_prep_call(zb, tabs, tab_t, mla_q_norm_g[l].reshape(1, -1), _pad_wq(mla_w_uq[l]).T.astype(BF16),
                                   mla_kv_norm_g[l].reshape(1, -1), wk.astype(BF16), wv.T.astype(BF16))
        o_mla = _mla_attn_call(qt, k, vt)

        lr = ret_log_rate[l]
        lr_vec = jnp.repeat(lr, HEAD_DIM, axis=1)
        lr_heads = jnp.broadcast_to(lr.reshape(8, 1), (8, 128))
        rf, rb = _ret_call(zc, lr_vec, lr_heads)

        wo = w_out[l].astype(BF16)
        x = _outproj_call(x, g2, yf, yb, pg, rwkv_ln_g[l].reshape(1, 256), rwkv_ln_b[l].reshape(1, 256), o_mla,
                          rf, rb, zc, ret_gn_g[l].reshape(1, 256), avg_bd,
                          wo[0:256], wo[256:768], wo[768:1024])

        x = _ffn_call(x, sh3, sc3, g3, w_ff2_in[l].astype(BF16), w_ff2_out[l].astype(BF16), final_g, l == L - 1)
    return x
```

```python
import functools
import math

import jax
import jax.numpy as jnp
from jax import lax
from jax.experimental import pallas as pl
from jax.experimental.pallas import tpu as pltpu

F32 = jnp.float32
BF16 = jnp.bfloat16
HIGHEST = lax.Precision.HIGHEST

D_MODEL = 1024
HEAD_DIM = 64
RWKV_WIDTH = 256
RWKV_HEADS = 4
RWKV_LORA = 64
RWKV_GATE_LORA = 128
RWKV_COLS = 1152
MLA_HEADS = 8
MLA_NOPE = 64
MLA_ROPE = 32
MLA_V = 64
MLA_Q_RANK = 384
MLA_KV_RANK = 256
MLA_WIDTH = 512
MLA_HEAD_PAD = 128
MLA_TQ = 2048
MLA_QCOLS = 512
MLA_TK = 512
MLA_ONES_ROWS = 16
MLA_BOUND_MARGIN = 1.01
MLA_BOUND_LIMIT = 50.0
RET_WIDTH = 256
RET_HEADS = 4
RET_CHUNK = 128
RET_BATCH_PER_STEP = 4
RWKV_CHUNK = 64
RWKV_BATCH_PER_STEP = 4
RWKV_PASSES = 1
D_FF = 2816
FFN_TM = 512
FFN_TF = 256
N_MOD = 9
ROPE_BASE = 10000.0
NORM_EPS = 1e-6
RWKV_LN_EPS = 64e-5
RET_LN_EPS = 1e-5

VMEM_LIMIT = 48 * 1024 * 1024


def _cparams(*sem):
    return pltpu.CompilerParams(dimension_semantics=sem, vmem_limit_bytes=VMEM_LIMIT)


def _sigmoid(x):
    return 1.0 / (1.0 + jnp.exp(-x))


def _dot(a, b):
    return jnp.dot(a, b, preferred_element_type=F32)


def _dot_hi(a, b):
    return jnp.dot(a, b, precision=HIGHEST, preferred_element_type=F32)


def _split(a):
    hi = a.astype(BF16)
    lo = (a - hi.astype(F32)).astype(BF16)
    return hi, lo


def _head_sums(a, sel):
    ah, al = _split(a)
    sel = sel.astype(BF16)
    return _dot(ah, sel) + _dot(al, sel)


def _dg3(a, b, dims):
    ah, al = _split(a)
    bh, bl = _split(b)
    dn = (dims, ((), ()))
    out = lax.dot_general(ah, bh, dn, preferred_element_type=F32)
    out = out + lax.dot_general(ah, bl, dn, preferred_element_type=F32)
    out = out + lax.dot_general(al, bh, dn, preferred_element_type=F32)
    return out


_NN = ((1,), (0,))
_NT = ((1,), (1,))
_TN = ((0,), (0,))


def _rms_mod(x, sh, sc):
    ms = jnp.mean(x * x, axis=-1, keepdims=True)
    return (x * lax.rsqrt(ms + NORM_EPS)) * (1.0 + sc) + sh


def _mod_kernel(c_ref, w_ref, b_ref, o_ref):
    c = c_ref[...]
    cond = c * _sigmoid(c)
    o_ref[0] = _dot_hi(cond, w_ref[0]) + b_ref[0]


def _mod_call(c, w_ada, b_ada):
    L, D, N = w_ada.shape
    B = c.shape[0]
    tn = 1152
    return pl.pallas_call(
        _mod_kernel,
        grid=(L, N // tn),
        in_specs=[
            pl.BlockSpec((B, D), lambda l, j: (0, 0)),
            pl.BlockSpec((1, D, tn), lambda l, j: (l, 0, j)),
            pl.BlockSpec((1, 1, tn), lambda l, j: (l, 0, j)),
        ],
        out_specs=pl.BlockSpec((1, B, tn), lambda l, j: (l, 0, j)),
        out_shape=jax.ShapeDtypeStruct((L, B, N), F32),
        compiler_params=_cparams("parallel", "parallel"),
        name="adaln_mod",
    )(c, w_ada, b_ada.reshape(L, 1, N))


def _rope_kernel(pos_ref, inv_ref, mc_ref, m1_ref, ma_ref, mb_ref, o_ref, ot_ref):
    pos = pos_ref[0].astype(F32)
    for i in range(2):
        ang = pos * inv_ref[i:i + 1, :]
        cos = jnp.cos(ang)
        sin = jnp.sin(ang)
        o_ref[i, 0, :, 0:128] = cos * mc_ref[i:i + 1, :] + m1_ref[i:i + 1, :]
        o_ref[i, 0, :, 128:256] = sin * ma_ref[i:i + 1, :]
        o_ref[i, 0, :, 256:384] = sin * mb_ref[i:i + 1, :]
    ot_ref[0] = o_ref[0, 0].T


def _rope_consts():
    lane = jnp.arange(128)
    inv_m = ROPE_BASE ** (-jnp.arange(0, MLA_ROPE, 2, dtype=F32) / MLA_ROPE)
    in_rope = (lane >= 64) & (lane < 96)
    inv0 = jnp.where(in_rope, inv_m[(lane - 64) % 16], 0.0)
    mc0 = in_rope.astype(F32)
    m10 = (lane < 64).astype(F32)
    ma0 = jnp.where((lane >= 64) & (lane < 80), -1.0, 0.0)
    mb0 = jnp.where((lane >= 80) & (lane < 96), 1.0, 0.0)
    inv_r = ROPE_BASE ** (-jnp.arange(0, HEAD_DIM, 2, dtype=F32) / HEAD_DIM)
    inv1 = inv_r[lane % 32]
    mc1 = jnp.ones((128,), F32)
    m11 = jnp.zeros((128,), F32)
    ma1 = jnp.where((lane % 64) < 32, -1.0, 0.0)
    mb1 = jnp.where((lane % 64) >= 32, 1.0, 0.0)
    st = lambda a, b: jnp.stack([a, b]).astype(F32)
    return st(inv0, inv1), st(mc0, mc1), st(m10, m11), st(ma0, ma1), st(mb0, mb1)


def _rope_call(positions):
    B, S = positions.shape
    ts = min(S, 512)
    consts = _rope_consts()
    cspec = pl.BlockSpec((2, 128), lambda b, i: (0, 0))
    return pl.pallas_call(
        _rope_kernel,
        grid=(B, S // ts),
        in_specs=[pl.BlockSpec((1, ts, 1), lambda b, i: (b, i, 0))] + [cspec] * 5,
        out_specs=[pl.BlockSpec((2, 1, ts, 384), lambda b, i: (0, b, i, 0)),
                   pl.BlockSpec((1, 384, ts), lambda b, i: (b, 0, i))],
        out_shape=[jax.ShapeDtypeStruct((2, B, S, 384), F32), jax.ShapeDtypeStruct((B, 384, S), F32)],
        compiler_params=_cparams("parallel", "parallel"),
        name="rope_tables",
    )(positions.reshape(B, S, 1), *consts)


def _apply_rope(x, tab, shift):
    n = x.shape[1] // 128
    width = x.shape[1]
    cc = jnp.tile(tab[:, 0:128], (1, n))
    sa = jnp.tile(tab[:, 128:256], (1, n))
    sb = jnp.tile(tab[:, 256:384], (1, n))
    return x * cc + pltpu.roll(x, width - shift, axis=1) * sa + pltpu.roll(x, shift, axis=1) * sb


def _ffn_kernel(x_ref, sh_ref, sc_ref, g_ref, wi_ref, wo_ref, fg_ref, o_ref, *, final):
    x = x_ref[0]
    h = _rms_mod(x, sh_ref[0], sc_ref[0]).astype(BF16)
    acc = None
    for c in range(D_FF // FFN_TF):
        cols = slice(c * FFN_TF, (c + 1) * FFN_TF)
        gate = _dot(h, wi_ref[:, cols])
        up = _dot(h, wi_ref[:, D_FF + c * FFN_TF:D_FF + (c + 1) * FFN_TF])
        act = (gate * _sigmoid(gate) * up).astype(BF16)
        part = _dot(act, wo_ref[cols, :])
        acc = part if acc is None else acc + part
    y = x + (0.5 * g_ref[0]) * acc
    if final:
        ms = jnp.mean(y * y, axis=-1, keepdims=True)
        y = (y * lax.rsqrt(ms + NORM_EPS)) * fg_ref[...]
    o_ref[0] = y


def _ffn_call(x, sh, sc, g, w_in_bf, w_out_bf, final_g, final):
    B, S, D = x.shape
    tm = min(S, FFN_TM)
    vec = pl.BlockSpec((1, 1, D), lambda b, i: (b, 0, 0))
    resident = lambda shape: pl.BlockSpec(shape, lambda b, i: (0, 0), pipeline_mode=pl.Buffered(1))
    return pl.pallas_call(
        functools.partial(_ffn_kernel, final=final),
        grid=(B, S // tm),
        in_specs=[
            pl.BlockSpec((1, tm, D), lambda b, i: (b, i, 0)),
            vec, vec, vec,
            resident((D, 2 * D_FF)), resident((D_FF, D)),
            pl.BlockSpec((1, D), lambda b, i: (0, 0)),
        ],
        out_specs=pl.BlockSpec((1, tm, D), lambda b, i: (b, i, 0)),
        out_shape=jax.ShapeDtypeStruct((B, S, D), F32),
        compiler_params=_cparams("parallel", "parallel"),
        name="ffn",
    )(x, sh, sc, g, w_in_bf, w_out_bf, final_g)


ZB_COLS = MLA_Q_RANK + MLA_KV_RANK + MLA_HEAD_PAD
ZC_COLS = 4 * RET_WIDTH
Z_COLS = RWKV_COLS + ZB_COLS + ZC_COLS


def _inproj_kernel(x_ref, sh_ref, sc_ref, w_ref, tab_ref, za_ref, zb_ref, zc_ref):
    h = _rms_mod(x_ref[0], sh_ref[0], sc_ref[0]).astype(BF16)
    z = _dot(h, w_ref[...])
    za_ref[0] = z[:, 0:RWKV_COLS]
    zb_ref[0] = z[:, RWKV_COLS:RWKV_COLS + ZB_COLS].astype(BF16)
    o = RWKV_COLS + ZB_COLS
    tab = tab_ref[0, 0]
    zc_ref[0, :, 0:256] = _apply_rope(z[:, o:o + 256], tab, 32).astype(BF16)
    zc_ref[0, :, 256:512] = (_apply_rope(z[:, o + 256:o + 512], tab, 32) * (HEAD_DIM ** -0.5)).astype(BF16)
    zc_ref[0, :, 512:1024] = z[:, o + 512:Z_COLS].astype(BF16)


def _inproj_call(x, sh, sc, wz_bf, tabs):
    B, S, D = x.shape
    tm = min(S, 512)
    vec = pl.BlockSpec((1, 1, D), lambda b, i: (b, 0, 0))
    out = lambda n: pl.BlockSpec((1, tm, n), lambda b, i: (b, i, 0))
    return pl.pallas_call(
        _inproj_kernel,
        grid=(B, S // tm),
        in_specs=[
            pl.BlockSpec((1, tm, D), lambda b, i: (b, i, 0)),
            vec, vec,
            pl.BlockSpec((D, Z_COLS), lambda b, i: (0, 0)),
            pl.BlockSpec((1, 1, tm, 384), lambda b, i: (1, b, i, 0)),
        ],
        out_specs=[out(RWKV_COLS), out(ZB_COLS), out(ZC_COLS)],
        out_shape=[jax.ShapeDtypeStruct((B, S, n), dt)
                   for n, dt in ((RWKV_COLS, F32), (ZB_COLS, BF16), (ZC_COLS, BF16))],
        compiler_params=_cparams("parallel", "parallel"),
        name="inproj",
    )(x, sh, sc, wz_bf, tabs)


def _rwkv_prep_kernel(z_ref, zp_ref, zn_ref, mu_ref, w0_ref, wup_ref, a0_ref, aup_ref, gup_ref,
                      kk_ref, ka_ref, rk_ref, ones_ref, pc_ref, pd_ref, pw_ref, pg_ref, *, ts, nt):
    i = pl.program_id(1)
    z = z_ref[0]
    row = lax.broadcasted_iota(jnp.int32, (ts, 1), 0)
    prev_edge = jnp.where(i > 0, zp_ref[0, 7:8, :], 0.0)
    next_edge = jnp.where(i < nt - 1, zn_ref[0, 0:1, :], 0.0)
    prev = jnp.where(row == 0, prev_edge, pltpu.roll(z, 1, axis=0))
    nxt = jnp.where(row == ts - 1, next_edge, pltpu.roll(z, ts - 1, axis=0))
    zs = z + mu_ref[...] * (0.5 * (prev + nxt) - z)

    r = zs[:, 0:256]
    k = zs[:, 256:512]
    v = zs[:, 512:768]
    g_lo = zs[:, 768:896]
    w_lo = zs[:, 896:1024]
    a_lo = zs[:, 1024:1152]

    w_raw = w0_ref[...] + _dg3(jnp.tanh(w_lo), wup_ref[...], _NN)
    logw = (-math.exp(-0.5)) * _sigmoid(w_raw)
    a = _sigmoid(a0_ref[...] + _dg3(a_lo, aup_ref[...], _NN))
    g = _dg3(_sigmoid(g_lo), gup_ref[...], _NN)

    ones_bd = ones_ref[...]
    kk0 = k * kk_ref[...]
    nrm = jnp.sqrt(_head_sums(kk0 * kk0, ones_bd))
    kk = kk0 / jnp.maximum(nrm, 1e-12)
    bonus = _head_sums(r * k * rk_ref[...], ones_bd) * v

    pc_ref[0, :, 0:256] = r.astype(BF16)
    pc_ref[0, :, 256:512] = v.astype(BF16)
    pc_ref[0, :, 512:768] = kk.astype(BF16)
    for d in range(2):
        a_d = a[:, 256 * d:256 * d + 256]
        pd_ref[d, 0, :, 0:256] = (k * (1.0 + (a_d - 1.0) * ka_ref[...])).astype(BF16)
        pd_ref[d, 0, :, 256:512] = (kk * a_d).astype(BF16)
        pw_ref[d, 0] = logw[:, 256 * d:256 * d + 256]
    pg_ref[0, :, 0:256] = g.astype(BF16)
    pg_ref[0, :, 256:512] = bonus.astype(BF16)


def _rwkv_prep_call(za, mu, w0, wup_bd, a0, aup_bd, gup, k_k, k_a, r_k, ones_bd):
    B, S, _ = za.shape
    ts = min(S, 512)
    nt = S // ts
    nb8 = S // 8
    const = lambda shape: pl.BlockSpec(shape, lambda b, i: (0,) * len(shape))
    return pl.pallas_call(
        functools.partial(_rwkv_prep_kernel, ts=ts, nt=nt),
        grid=(B, nt),
        in_specs=[
            pl.BlockSpec((1, ts, RWKV_COLS), lambda b, i: (b, i, 0)),
            pl.BlockSpec((1, 8, RWKV_COLS), lambda b, i: (b, jnp.maximum(i * (ts // 8) - 1, 0), 0)),
            pl.BlockSpec((1, 8, RWKV_COLS), lambda b, i: (b, jnp.minimum((i + 1) * (ts // 8), nb8 - 1), 0)),
            const((1, RWKV_COLS)),
            const((1, 512)), const((128, 512)), const((1, 512)), const((128, 512)), const((128, 256)),
            const((1, 256)), const((1, 256)), const((1, 256)), const((256, 256)),
        ],
        out_specs=[
            pl.BlockSpec((1, ts, 768), lambda b, i: (b, i, 0)),
            pl.BlockSpec((2, 1, ts, 512), lambda b, i: (0, b, i, 0)),
            pl.BlockSpec((2, 1, ts, 256), lambda b, i: (0, b, i, 0)),
            pl.BlockSpec((1, ts, 512), lambda b, i: (b, i, 0)),
        ],
        out_shape=[
            jax.ShapeDtypeStruct((B, S, 768), BF16),
            jax.ShapeDtypeStruct((2, B, S, 512), BF16),
            jax.ShapeDtypeStruct((2, B, S, 256), F32),
            jax.ShapeDtypeStruct((B, S, 512), BF16),
        ],
        compiler_params=_cparams("parallel", "parallel"),
        name="rwkv_prep",
    )(za, za, za, mu, w0, wup_bd, a0, aup_bd, gup, k_k, k_a, r_k, ones_bd)


def _rwkv_dir_operands(direction, pc, pd, logw):
    C = RWKV_CHUNK
    r = pc[:, 0:256]
    v = pc[:, 256:512]
    kk = pc[:, 512:768]
    kmod = pd[:, 0:256]
    kka = pd[:, 256:512]

    row = lax.broadcasted_iota(jnp.int32, (C, C), 0)
    col = lax.broadcasted_iota(jnp.int32, (C, C), 1)
    tri = jnp.where((col <= row) if direction == 0 else (col >= row), 1.0, 0.0).astype(BF16)
    lw_hi, lw_lo = _split(logw)
    l_in = _dot(tri, lw_hi) + _dot(tri, lw_lo)
    l_ex = l_in - logw
    l_tot = l_in[C - 1:C, :] if direction == 0 else l_in[0:1, :]
    e_ex = jnp.exp(l_ex)
    e_neg = jnp.exp(-l_in)
    e_tot = jnp.exp(l_tot - l_in)
    kkg = kk * e_ex
    bbar = kka * e_neg
    kbar = kmod * e_neg
    bg = kka * e_tot
    kg = kmod * e_tot
    rg = r * jnp.exp(l_in) if direction == 0 else r * e_ex
    g_tot = jnp.exp(l_tot)
    return dict(kkg=kkg, bbar=bbar, kbar=kbar, bg=bg, kg=kg, rg=rg, v=v, g_tot=g_tot)


def _operand(a, passes):
    hi = a.astype(BF16)
    lo = (a - hi.astype(F32)).astype(BF16) if passes > 1 else None
    return hi, lo


def _mm(a, b, dims):
    dn = (dims, ((), ()))
    out = lax.dot_general(a[0], b[0], dn, preferred_element_type=F32)
    if b[1] is not None:
        out = out + lax.dot_general(a[0], b[1], dn, preferred_element_type=F32)
    if a[1] is not None:
        out = out + lax.dot_general(a[1], b[0], dn, preferred_element_type=F32)
    return out


def _rwkv_scan_kernel(pcf_ref, pcb_ref, pdf_ref, pdb_ref, pwf_ref, pwb_ref, yf_ref, yb_ref, s_ref):
    C = RWKV_CHUNK
    P = RWKV_PASSES

    @pl.when(pl.program_id(1) == 0)
    def _():
        s_ref[...] = jnp.zeros_like(s_ref)

    row = lax.broadcasted_iota(jnp.int32, (C, C), 0)
    col = lax.broadcasted_iota(jnp.int32, (C, C), 1)
    eye = col == row
    ident = jnp.where(eye, 1.0, 0.0)
    nb = pcf_ref.shape[0]
    ops = {}
    for bb in range(nb):
        ops[bb, 0] = _rwkv_dir_operands(0, pcf_ref[bb].astype(F32), pdf_ref[0, bb].astype(F32), pwf_ref[0, bb])
        ops[bb, 1] = _rwkv_dir_operands(1, pcb_ref[bb].astype(F32), pdb_ref[0, bb].astype(F32), pwb_ref[0, bb])
    earlier = (col < row, col > row)
    ymask = (col <= row, col > row)
    units = [(bb, d, h) for bb in range(nb) for d in range(2) for h in range(RWKV_HEADS)]
    nu = len(units)
    sl = lambda h: slice(64 * h, 64 * h + 64)
    s0 = [s_ref[u] for u in units]

    def head(name):
        return [_operand(ops[bb, d][name][:, sl(h)], P) for bb, d, h in units]

    kkg, bbar, kbar, bg, kg, rg, v = (head(n) for n in ("kkg", "bbar", "kbar", "bg", "kg", "rg", "v"))
    akk = [jnp.where(earlier[d], _mm(kkg[i], bbar[i], _NT), 0.0) for i, (_, d, _) in enumerate(units)]
    bk = [jnp.where(earlier[d], _mm(kkg[i], kbar[i], _NT), 0.0) for i, (_, d, _) in enumerate(units)]
    ark = [jnp.where(ymask[d], _mm(rg[i], bbar[i], _NT), 0.0) for i, (_, d, _) in enumerate(units)]
    brk = [jnp.where(ymask[d], _mm(rg[i], kbar[i], _NT), 0.0) for i, (_, d, _) in enumerate(units)]
    pw = [-a for a in akk]
    t_inv = [ident + x for x in pw]
    for _ in range(int(math.log2(C)) - 1):
        pw_o = [_operand(x, P) for x in pw]
        pw = [_mm(x, x, _NN) for x in pw_o]
        pw_o = [_operand(x, P) for x in pw]
        t_inv = [t + _mm(_operand(t, P), x, _NN) for t, x in zip(t_inv, pw_o)]
    t_o = [_operand(t, P) for t in t_inv]
    bkv = [_mm(_operand(b, P), x, _NN) for b, x in zip(bk, v)]
    w_m = [_mm(t, x, _NN) for t, x in zip(t_o, kkg)]
    u_m = [_mm(t, _operand(x, P), _NN) for t, x in zip(t_o, bkv)]
    w_o = [_operand(x, P) for x in w_m]
    u_o = [_operand(x, P) for x in u_m]
    ark_o = [_operand(x, P) for x in ark]
    brk_o = [_operand(x, P) for x in brk]
    m_m = [jnp.where(eye, ops[bb, d]["g_tot"][:, sl(h)], 0.0) - _mm(w_o[i], bg[i], _TN)
           for i, (bb, d, h) in enumerate(units)]
    n_m = [_mm(v[i], kg[i], _TN) - _mm(u_o[i], bg[i], _TN) for i in range(nu)]
    q_m = [ops[bb, d]["rg"][:, sl(h)] - _mm(ark_o[i], w_o[i], _NN) for i, (bb, d, h) in enumerate(units)]
    y_i = [_mm(brk_o[i], v[i], _NN) - _mm(ark_o[i], u_o[i], _NN) for i in range(nu)]
    s_o = [_operand(s, P) for s in s0]
    y = [y_i[i] + _mm(_operand(q_m[i], P), s_o[i], _NT) for i in range(nu)]
    s_new = [_mm(s_o[i], _operand(m_m[i], P), _NN) + n_m[i] for i in range(nu)]
    for i, (bb, d, h) in enumerate(units):
        (yf_ref if d == 0 else yb_ref)[bb, :, sl(h)] = y[i]
        s_ref[bb, d, h] = s_new[i]


def _rwkv_scan_call(pc, pd, pw):
    B, S, _ = pc.shape
    C = RWKV_CHUNK
    nc = S // C
    nb = math.gcd(B, RWKV_BATCH_PER_STEP)
    return pl.pallas_call(
        _rwkv_scan_kernel,
        grid=(B // nb, nc),
        in_specs=[
            pl.BlockSpec((nb, C, 768), lambda b, c: (b, c, 0)),
            pl.BlockSpec((nb, C, 768), lambda b, c: (b, nc - 1 - c, 0)),
            pl.BlockSpec((1, nb, C, 512), lambda b, c: (0, b, c, 0)),
            pl.BlockSpec((1, nb, C, 512), lambda b, c: (1, b, nc - 1 - c, 0)),
            pl.BlockSpec((1, nb, C, 256), lambda b, c: (0, b, c, 0)),
            pl.BlockSpec((1, nb, C, 256), lambda b, c: (1, b, nc - 1 - c, 0)),
        ],
        out_specs=[
            pl.BlockSpec((nb, C, 256), lambda b, c: (b, c, 0)),
            pl.BlockSpec((nb, C, 256), lambda b, c: (b, nc - 1 - c, 0)),
        ],
        out_shape=[jax.ShapeDtypeStruct((B, S, 256), F32)] * 2,
        scratch_shapes=[pltpu.VMEM((nb, 2, RWKV_HEADS, HEAD_DIM, HEAD_DIM), F32)],
        compiler_params=_cparams("parallel", "arbitrary"),
        name="rwkv_scan",
    )(pc, pc, pd, pd, pw, pw)


def _mla_prep_kernel(zb_ref, tab_ref, tabt_ref, qg_ref, wqt_ref, kg_ref, wk_ref, wvt_ref, hsel_ref,
                     qt_ref, k_ref, vt_ref, kn_ref):
    zb = zb_ref[0].astype(F32)
    tab = tab_ref[0, 0]
    cq = zb[:, 0:MLA_Q_RANK]
    ckv = zb[:, MLA_Q_RANK:MLA_Q_RANK + MLA_KV_RANK]
    kpe = zb[:, MLA_Q_RANK + MLA_KV_RANK:ZB_COLS]

    def rms(t):
        return t * lax.rsqrt(jnp.mean(t * t, axis=-1, keepdims=True) + NORM_EPS)

    cqn = (rms(cq) * qg_ref[...]).astype(BF16)
    ckvn = (rms(ckv) * kg_ref[...]).astype(BF16)
    nt_dims = (_NT, ((), ()))
    qt = lax.dot_general(wqt_ref[...], cqn, nt_dims, preferred_element_type=F32)
    rows = qt.shape[0]
    tabt = tabt_ref[0]
    cc, sa, sb = (jnp.tile(tabt[128 * j:128 * j + 128], (MLA_HEADS, 1)) for j in range(3))
    qt = qt * cc + pltpu.roll(qt, rows - 16, axis=0) * sa + pltpu.roll(qt, 16, axis=0) * sb
    k = _dot(ckvn, wk_ref[...])
    kpe = _apply_rope(kpe, tab, 16)
    k = k + jnp.tile(kpe, (1, MLA_HEADS))
    qt_ref[0] = (qt * (math.log2(math.e) * (MLA_NOPE + MLA_ROPE) ** -0.5)).astype(BF16)
    kb = k.astype(BF16)
    k_ref[0] = kb
    kf = kb.astype(F32)
    k_sq = jnp.max(_head_sums(kf * kf, hsel_ref[...]), axis=0, keepdims=True)
    kn_ref[0, 0] = jnp.broadcast_to(k_sq, (8, 128))
    vt_ref[0, 0] = lax.dot_general(wvt_ref[...], ckvn, nt_dims, preferred_element_type=F32).astype(BF16)


def _mla_key_tile(S):
    return min(S, MLA_TK)


def _mla_prep_call(zb, tabs, tab_t, qg, wqt_bf, kg, wk_bf, wvt_bf, head_sel):
    B, S, _ = zb.shape
    ts = _mla_key_tile(S)
    HP = MLA_HEADS * MLA_HEAD_PAD
    const = lambda shape: pl.BlockSpec(shape, lambda b, i: (0,) * len(shape))
    return pl.pallas_call(
        _mla_prep_kernel,
        grid=(B, S // ts),
        in_specs=[
            pl.BlockSpec((1, ts, ZB_COLS), lambda b, i: (b, i, 0)),
            pl.BlockSpec((1, 1, ts, 384), lambda b, i: (0, b, i, 0)),
            pl.BlockSpec((1, 384, ts), lambda b, i: (b, 0, i)),
            const((1, MLA_Q_RANK)), const((HP, MLA_Q_RANK)),
            const((1, MLA_KV_RANK)), const((MLA_KV_RANK, HP)), const((MLA_WIDTH, MLA_KV_RANK)),
            const((HP, 128)),
        ],
        out_specs=[
            pl.BlockSpec((1, HP, ts), lambda b, i: (b, 0, i)),
            pl.BlockSpec((1, ts, HP), lambda b, i: (b, i, 0)),
            pl.BlockSpec((1, 1, MLA_WIDTH, ts), lambda b, i: (b, i, 0, 0)),
            pl.BlockSpec((1, 1, 8, 128), lambda b, i: (b, i, 0, 0)),
        ],
        out_shape=[
            jax.ShapeDtypeStruct((B, HP, S), BF16),
            jax.ShapeDtypeStruct((B, S, HP), BF16),
            jax.ShapeDtypeStruct((B, S // ts, MLA_WIDTH, ts), BF16),
            jax.ShapeDtypeStruct((B, S // ts, 8, 128), F32),
        ],
        compiler_params=_cparams("parallel", "parallel"),
        name="mla_prep",
    )(zb, tabs, tab_t, qg, wqt_bf, kg, wk_bf, wvt_bf, head_sel)


def _mla_attn_kernel(qt_ref, k_ref, vt_ref, kn_ref, o_ref, *, tk, nk):
    tq = qt_ref.shape[2]
    cw = min(tq, MLA_QCOLS)
    ncol = tq // cw
    chunks = [(hh, c) for hh in range(2) for c in range(ncol)]
    ones = jnp.ones((MLA_ONES_ROWS, tk), BF16)
    acc0 = jnp.zeros((MLA_V + MLA_ONES_ROWS, cw), F32)

    def scores(i):
        ks = pl.multiple_of(i * tk, tk)
        return [_dot(k_ref[0, pl.ds(ks, tk), 128 * hh:128 * hh + 128],
                     qt_ref[0, 128 * hh:128 * hh + 128, c * cw:(c + 1) * cw]) for hh, c in chunks]

    def values(i, hh):
        return jnp.concatenate([vt_ref[0, i, 64 * hh:64 * hh + 64, :], ones], axis=0)

    k_sq = jnp.max(kn_ref[0], axis=(0, 1), keepdims=True)[0]
    lane = lax.broadcasted_iota(jnp.int32, k_sq.shape, 1)
    head0 = 2 * pl.program_id(1)
    bounds = []
    for hh, c in chunks:
        k_h = jnp.sum(jnp.where(lane == head0 + hh, k_sq, 0.0), axis=1, keepdims=True)
        qf = qt_ref[0, 128 * hh:128 * hh + 128, c * cw:(c + 1) * cw].astype(F32)
        bounds.append(jnp.sqrt(jnp.sum(qf * qf, axis=0, keepdims=True) * k_h) * MLA_BOUND_MARGIN)
    worst = functools.reduce(jnp.maximum, [jnp.max(b) for b in bounds])

    def bounded_offset():
        def body(i, carry):
            ps = [jnp.exp2(s - bounds[j]).astype(BF16) for j, s in enumerate(scores(i))]
            return tuple(carry[j] + _dot(values(i, hh), ps[j]) for j, (hh, c) in enumerate(chunks))

        return lax.fori_loop(0, nk, body, (acc0,) * len(chunks))

    def running_max():
        def body(i, carry):
            stats = []
            for j, s in enumerate(scores(i)):
                m = carry[j][0]
                m_new = jnp.maximum(m, jnp.max(s, axis=0, keepdims=True))
                stats.append((m_new, jnp.exp2(m - m_new), jnp.exp2(s - m_new).astype(BF16)))
            new = []
            for j, (hh, c) in enumerate(chunks):
                m_new, alpha, p = stats[j]
                new.append((m_new, alpha * carry[j][1] + _dot(values(i, hh), p)))
            return tuple(new)

        init = (jnp.full((1, cw), -jnp.inf, F32), acc0)
        return tuple(r[1] for r in lax.fori_loop(0, nk, body, (init,) * len(chunks)))

    res = lax.cond(worst <= MLA_BOUND_LIMIT, bounded_offset, running_max)
    out_t = [jnp.concatenate([res[hh * ncol + c][0:MLA_V] / res[hh * ncol + c][MLA_V:MLA_V + 1]
                              for c in range(ncol)], axis=1) for hh in range(2)]
    o_ref[0] = jnp.concatenate(out_t, axis=0).T.astype(BF16)


def _mla_attn_call(qt, k, vt, kn):
    B, S, _ = k.shape
    tq = min(S, MLA_TQ)
    tk = _mla_key_tile(S)
    nk = S // tk
    return pl.pallas_call(
        functools.partial(_mla_attn_kernel, tk=tk, nk=nk),
        grid=(B, MLA_HEADS // 2, S // tq),
        in_specs=[
            pl.BlockSpec((1, 256, tq), lambda b, h, i: (b, h, i)),
            pl.BlockSpec((1, S, 256), lambda b, h, i: (b, 0, h)),
            pl.BlockSpec((1, nk, 128, tk), lambda b, h, i: (b, 0, h, 0)),
            pl.BlockSpec((1, nk, 8, 128), lambda b, h, i: (b, 0, 0, 0)),
        ],
        out_specs=pl.BlockSpec((1, tq, 128), lambda b, h, i: (b, i, h)),
        out_shape=jax.ShapeDtypeStruct((B, S, MLA_WIDTH), BF16),
        compiler_params=_cparams("parallel", "parallel", "arbitrary"),
        name="mla_attn",
    )(qt, k, vt, kn)


def _ret_dir_operands(direction, zc, lrv, lrh):
    C = RET_CHUNK
    q = zc[:, 0:256]
    k = zc[:, 256:512]
    v = zc[:, 512:768]
    lgv = -jnp.exp(lrv)
    lgh = -jnp.exp(lrh)
    pos = lax.broadcasted_iota(jnp.int32, (C, 1), 0).astype(F32)
    row = lax.broadcasted_iota(jnp.int32, (C, C), 0)
    col = lax.broadcasted_iota(jnp.int32, (C, C), 1)
    if direction == 0:
        kw = k * jnp.exp(lgv * (C - 1.0 - pos))
        qw = q * jnp.exp(lgv * (pos + 1.0))
        mask = col <= row
        dist = (row - col).astype(F32)
    else:
        kw = k * jnp.exp(lgv * pos)
        qw = q * jnp.exp(lgv * (C - pos))
        mask = col > row
        dist = (col - row).astype(F32)
    dist = jnp.maximum(dist, 0.0)
    dmats = [jnp.where(mask, jnp.exp(lgh[h:h + 1, :] * dist), 0.0) for h in range(RET_HEADS)]
    decays = [jnp.exp(lgh[h:h + 1, 0:64] * C) for h in range(RET_HEADS)]
    qb, kb, vb, kwb, qwb = (t.astype(BF16) for t in (q, k, v, kw, qw))
    return dict(q=qb, k=kb, v=vb, kw=kwb, qw=qwb, dmat=dmats, decay=decays)


def _ret_kernel(zf_ref, zb_ref, lrv_ref, lrh_ref, yf_ref, yb_ref, r_ref):
    @pl.when(pl.program_id(1) == 0)
    def _():
        r_ref[...] = jnp.zeros_like(r_ref)

    nb = zf_ref.shape[0]
    ops = {}
    for bb in range(nb):
        ops[bb, 0] = _ret_dir_operands(0, zf_ref[bb].astype(F32), lrv_ref[0:1, :], lrh_ref[0:4, :])
        ops[bb, 1] = _ret_dir_operands(1, zb_ref[bb].astype(F32), lrv_ref[1:2, :], lrh_ref[4:8, :])
    units = [(bb, d, h) for bb in range(nb) for d in range(2) for h in range(RET_HEADS)]
    sl = lambda h: slice(64 * h, 64 * h + 64)
    r0 = [r_ref[u] for u in units]
    sc = [lax.dot_general(ops[bb, d]["q"][:, sl(h)], ops[bb, d]["k"][:, sl(h)], (_NT, ((), ())),
                          preferred_element_type=F32) * ops[bb, d]["dmat"][h] for bb, d, h in units]
    kv = [lax.dot_general(ops[bb, d]["kw"][:, sl(h)], ops[bb, d]["v"][:, sl(h)], (_TN, ((), ())),
                          preferred_element_type=F32) for bb, d, h in units]
    cross = [_dot(ops[bb, d]["qw"][:, sl(h)], r0[i].astype(BF16)) for i, (bb, d, h) in enumerate(units)]
    inner = [_dot(sc[i].astype(BF16), ops[bb, d]["v"][:, sl(h)]) for i, (bb, d, h) in enumerate(units)]
    for i, (bb, d, h) in enumerate(units):
        (yf_ref if d == 0 else yb_ref)[bb, :, sl(h)] = inner[i] + cross[i]
        r_ref[bb, d, h] = r0[i] * ops[bb, d]["decay"][h] + kv[i]


def _ret_call(zc, lr_vec, lr_heads):
    B, S, _ = zc.shape
    C = RET_CHUNK
    nc = S // C
    nb = math.gcd(B, RET_BATCH_PER_STEP)
    const = lambda shape: pl.BlockSpec(shape, lambda b, c: (0,) * len(shape))
    return pl.pallas_call(
        _ret_kernel,
        grid=(B // nb, nc),
        in_specs=[
            pl.BlockSpec((nb, C, ZC_COLS), lambda b, c: (b, c, 0)),
            pl.BlockSpec((nb, C, ZC_COLS), lambda b, c: (b, nc - 1 - c, 0)),
            const((2, 256)), const((8, 128)),
        ],
        out_specs=[
            pl.BlockSpec((nb, C, 256), lambda b, c: (b, c, 0)),
            pl.BlockSpec((nb, C, 256), lambda b, c: (b, nc - 1 - c, 0)),
        ],
        out_shape=[jax.ShapeDtypeStruct((B, S, 256), F32)] * 2,
        scratch_shapes=[pltpu.VMEM((nb, 2, RET_HEADS, HEAD_DIM, HEAD_DIM), F32)],
        compiler_params=_cparams("parallel", "arbitrary"),
        name="retention",
    )(zc, zc, lr_vec, lr_heads)


def _head_norm(y, avg_bd, eps):
    mu = _head_sums(y, avg_bd)
    d = y - mu
    var = _head_sums(d * d, avg_bd)
    return d * lax.rsqrt(var + eps)


def _outproj_kernel(x_ref, g2_ref, yf_ref, yb_ref, pg_ref, lng_ref, lnb_ref, om_ref, rf_ref, rb_ref,
                    gate_ref, gng_ref, avg_ref, wa_ref, wb_ref, wc_ref, o_ref):
    avg_bd = avg_ref[...]
    y = _head_norm(yf_ref[0] + yb_ref[0], avg_bd, RWKV_LN_EPS) * lng_ref[...] + lnb_ref[...]
    pg = pg_ref[0].astype(F32)
    o_a = (y + pg[:, 256:512]) * pg[:, 0:256]
    yr = _head_norm(rf_ref[0] + rb_ref[0], avg_bd, RET_LN_EPS) * gng_ref[...]
    gate = gate_ref[0].astype(F32)
    o_c = (gate * _sigmoid(gate)) * yr
    mixed = _dot(o_a.astype(BF16), wa_ref[...])
    mixed = mixed + _dot(om_ref[0], wb_ref[...])
    mixed = mixed + _dot(o_c.astype(BF16), wc_ref[...])
    o_ref[0] = x_ref[0] + g2_ref[0] * mixed


def _outproj_call(x, g2, yf, yb, pg, ln_g, ln_b, o_mla, rf, rb, zc, gn_g, avg_bd, wa, wb, wc):
    B, S, D = x.shape
    tm = min(S, 512)
    tok = lambda n: pl.BlockSpec((1, tm, n), lambda b, i: (b, i, 0))
    const = lambda shape: pl.BlockSpec(shape, lambda b, i: (0,) * len(shape))
    return pl.pallas_call(
        _outproj_kernel,
        grid=(B, S // tm),
        in_specs=[
            tok(D),
            pl.BlockSpec((1, 1, D), lambda b, i: (b, 0, 0)),
            tok(256), tok(256), tok(512), const((1, 256)), const((1, 256)),
            tok(512), tok(256), tok(256),
            pl.BlockSpec((1, tm, 256), lambda b, i: (b, i, 3)),
            const((1, 256)), const((256, 256)),
            const((256, D)), const((512, D)), const((256, D)),
        ],
        out_specs=tok(D),
        out_shape=jax.ShapeDtypeStruct((B, S, D), F32),
        compiler_params=_cparams("parallel", "parallel"),
        name="outproj",
    )(x, g2, yf, yb, pg, ln_g, ln_b, o_mla, rf, rb, zc, gn_g, avg_bd, wa, wb, wc)


def _block_diag2(w):
    z = jnp.zeros_like(w[0])
    return jnp.concatenate([jnp.concatenate([w[0], z], axis=1), jnp.concatenate([z, w[1]], axis=1)], axis=0)


def _pad_inproj(w):
    D = w.shape[0]
    o = RWKV_COLS + MLA_Q_RANK + MLA_KV_RANK
    return jnp.concatenate(
        [w[:, :o], jnp.zeros((D, 64), w.dtype), w[:, o:o + MLA_ROPE], jnp.zeros((D, 32), w.dtype), w[:, o + MLA_ROPE:]],
        axis=1)


def _pad_wq(w):
    r = w.shape[0]
    w = w.reshape(r, MLA_HEADS, MLA_NOPE + MLA_ROPE)
    w = jnp.concatenate([w, jnp.zeros((r, MLA_HEADS, 32), w.dtype)], axis=2)
    return w.reshape(r, MLA_HEADS * MLA_HEAD_PAD)


def _split_wkv(w):
    r = w.shape[0]
    w = w.reshape(r, MLA_HEADS, MLA_NOPE + MLA_V)
    wk = jnp.concatenate([w[:, :, :MLA_NOPE], jnp.zeros((r, MLA_HEADS, 64), w.dtype)], axis=2)
    return wk.reshape(r, MLA_HEADS * MLA_HEAD_PAD), w[:, :, MLA_NOPE:].reshape(r, MLA_WIDTH)


def kernel(x, c, positions, w_ada, b_ada, w_ff1_in, w_ff1_out, w_ff2_in, w_ff2_out, w_in, w_out, rwkv_mu, rwkv_w0,
           rwkv_w_up, rwkv_a0, rwkv_a_up, rwkv_g_up, rwkv_k_k, rwkv_k_a, rwkv_r_k, rwkv_ln_g, rwkv_ln_b,
           mla_q_norm_g, mla_w_uq, mla_kv_norm_g, mla_w_ukv, ret_log_rate, ret_gn_g, final_norm_g):
    B, S, D = x.shape
    L = w_ada.shape[0]
    mod = _mod_call(c, w_ada, b_ada)
    tabs, tab_t = _rope_call(positions)

    head_id = jnp.arange(256) // HEAD_DIM
    same_head = (head_id[:, None] == head_id[None, :]).astype(F32)
    ones_bd = same_head
    avg_bd = same_head / HEAD_DIM
    final_g = final_norm_g.reshape(1, D)
    head_sel = (jnp.arange(MLA_HEADS * MLA_HEAD_PAD)[:, None] // MLA_HEAD_PAD == jnp.arange(128)[None, :]).astype(F32)

    for l in range(L):
        m = [mod[l, :, i * D:(i + 1) * D].reshape(B, 1, D) for i in range(N_MOD)]
        sh1, sc1, g1, sh2, sc2, g2, sh3, sc3, g3 = m

        x = _ffn_call(x, sh1, sc1, g1, w_ff1_in[l].astype(BF16), w_ff1_out[l].astype(BF16), final_g, False)

        za, zb, zc = _inproj_call(x, sh2, sc2, _pad_inproj(w_in[l]).astype(BF16), tabs)

        pc, pd, pw, pg = _rwkv_prep_call(
            za, rwkv_mu[l].reshape(1, -1), rwkv_w0[l].reshape(1, 512), _block_diag2(rwkv_w_up[l]),
            rwkv_a0[l].reshape(1, 512), _block_diag2(rwkv_a_up[l]), rwkv_g_up[l],
            rwkv_k_k[l].reshape(1, 256), rwkv_k_a[l].reshape(1, 256), rwkv_r_k[l].reshape(1, 256), ones_bd)
        yf, yb = _rwkv_scan_call(pc, pd, pw)

        wk, wv = _split_wkv(mla_w_ukv[l])
        qt, k, vt, kn = _mla_prep_call(zb, tabs, tab_t, mla_q_norm_g[l].reshape(1, -1),
                                       _pad_wq(mla_w_uq[l]).T.astype(BF16), mla_kv_norm_g[l].reshape(1, -1),
                                       wk.astype(BF16), wv.T.astype(BF16), head_sel)
        o_mla = _mla_attn_call(qt, k, vt, kn)

        lr = ret_log_rate[l]
        lr_vec = jnp.repeat(lr, HEAD_DIM, axis=1)
        lr_heads = jnp.broadcast_to(lr.reshape(8, 1), (8, 128))
        rf, rb = _ret_call(zc, lr_vec, lr_heads)

        wo = w_out[l].astype(BF16)
        x = _outproj_call(x, g2, yf, yb, pg, rwkv_ln_g[l].reshape(1, 256), rwkv_ln_b[l].reshape(1, 256), o_mla,
                          rf, rb, zc, ret_gn_g[l].reshape(1, 256), avg_bd,
                          wo[0:256], wo[256:768], wo[768:1024])

        x = _ffn_call(x, sh3, sc3, g3, w_ff2_in[l].astype(BF16), w_ff2_out[l].astype(BF16), final_g, l == L - 1)
    return x
```

```python
import functools
import math

import jax
import jax.numpy as jnp
from jax import lax
from jax.experimental import pallas as pl
from jax.experimental.pallas import tpu as pltpu

F32 = jnp.float32
BF16 = jnp.bfloat16
HIGHEST = lax.Precision.HIGHEST

D_MODEL = 1024
HEAD_DIM = 64
RWKV_WIDTH = 256
RWKV_HEADS = 4
RWKV_LORA = 64
RWKV_GATE_LORA = 128
RWKV_COLS = 1152
MLA_HEADS = 8
MLA_NOPE = 64
MLA_ROPE = 32
MLA_V = 64
MLA_Q_RANK = 384
MLA_KV_RANK = 256
MLA_WIDTH = 512
MLA_HEAD_PAD = 128
MLA_TQ = 2048
MLA_QCOLS = 512
MLA_TK = 512
MLA_ONES_ROWS = 16
MLA_BOUNDED_UNROLL = 4
MLA_BOUND_MARGIN = 1.01
MLA_BOUND_LIMIT = 50.0
RET_WIDTH = 256
RET_HEADS = 4
RET_CHUNK = 128
RET_BATCH_PER_STEP = 4
RWKV_CHUNK = 64
RWKV_BATCH_PER_STEP = 4
RWKV_PASSES = 1
D_FF = 2816
FFN_TM = 512
FFN_TF = 256
N_MOD = 9
ROPE_BASE = 10000.0
NORM_EPS = 1e-6
RWKV_LN_EPS = 64e-5
RET_LN_EPS = 1e-5

VMEM_LIMIT = 48 * 1024 * 1024


def _cparams(*sem):
    return pltpu.CompilerParams(dimension_semantics=sem, vmem_limit_bytes=VMEM_LIMIT)


def _sigmoid(x):
    return 1.0 / (1.0 + jnp.exp(-x))


def _dot(a, b):
    return jnp.dot(a, b, preferred_element_type=F32)


def _dot_hi(a, b):
    return jnp.dot(a, b, precision=HIGHEST, preferred_element_type=F32)


def _split(a):
    hi = a.astype(BF16)
    lo = (a - hi.astype(F32)).astype(BF16)
    return hi, lo


def _head_sums(a, sel):
    ah, al = _split(a)
    sel = sel.astype(BF16)
    return _dot(ah, sel) + _dot(al, sel)


def _dg3(a, b, dims):
    ah, al = _split(a)
    bh, bl = _split(b)
    dn = (dims, ((), ()))
    out = lax.dot_general(ah, bh, dn, preferred_element_type=F32)
    out = out + lax.dot_general(ah, bl, dn, preferred_element_type=F32)
    out = out + lax.dot_general(al, bh, dn, preferred_element_type=F32)
    return out


_NN = ((1,), (0,))
_NT = ((1,), (1,))
_TN = ((0,), (0,))


def _rms_mod(x, sh, sc):
    ms = jnp.mean(x * x, axis=-1, keepdims=True)
    return (x * lax.rsqrt(ms + NORM_EPS)) * (1.0 + sc) + sh


def _mod_kernel(c_ref, w_ref, b_ref, o_ref):
    c = c_ref[...]
    cond = c * _sigmoid(c)
    o_ref[0] = _dot_hi(cond, w_ref[0]) + b_ref[0]


def _mod_call(c, w_ada, b_ada):
    L, D, N = w_ada.shape
    B = c.shape[0]
    tn = 1152
    return pl.pallas_call(
        _mod_kernel,
        grid=(L, N // tn),
        in_specs=[
            pl.BlockSpec((B, D), lambda l, j: (0, 0)),
            pl.BlockSpec((1, D, tn), lambda l, j: (l, 0, j)),
            pl.BlockSpec((1, 1, tn), lambda l, j: (l, 0, j)),
        ],
        out_specs=pl.BlockSpec((1, B, tn), lambda l, j: (l, 0, j)),
        out_shape=jax.ShapeDtypeStruct((L, B, N), F32),
        compiler_params=_cparams("parallel", "parallel"),
        name="adaln_mod",
    )(c, w_ada, b_ada.reshape(L, 1, N))


def _rope_kernel(pos_ref, inv_ref, mc_ref, m1_ref, ma_ref, mb_ref, o_ref, ot_ref):
    pos = pos_ref[0].astype(F32)
    for i in range(2):
        ang = pos * inv_ref[i:i + 1, :]
        cos = jnp.cos(ang)
        sin = jnp.sin(ang)
        o_ref[i, 0, :, 0:128] = cos * mc_ref[i:i + 1, :] + m1_ref[i:i + 1, :]
        o_ref[i, 0, :, 128:256] = sin * ma_ref[i:i + 1, :]
        o_ref[i, 0, :, 256:384] = sin * mb_ref[i:i + 1, :]
    ot_ref[0] = o_ref[0, 0].T


def _rope_consts():
    lane = jnp.arange(128)
    inv_m = ROPE_BASE ** (-jnp.arange(0, MLA_ROPE, 2, dtype=F32) / MLA_ROPE)
    in_rope = (lane >= 64) & (lane < 96)
    inv0 = jnp.where(in_rope, inv_m[(lane - 64) % 16], 0.0)
    mc0 = in_rope.astype(F32)
    m10 = (lane < 64).astype(F32)
    ma0 = jnp.where((lane >= 64) & (lane < 80), -1.0, 0.0)
    mb0 = jnp.where((lane >= 80) & (lane < 96), 1.0, 0.0)
    inv_r = ROPE_BASE ** (-jnp.arange(0, HEAD_DIM, 2, dtype=F32) / HEAD_DIM)
    inv1 = inv_r[lane % 32]
    mc1 = jnp.ones((128,), F32)
    m11 = jnp.zeros((128,), F32)
    ma1 = jnp.where((lane % 64) < 32, -1.0, 0.0)
    mb1 = jnp.where((lane % 64) >= 32, 1.0, 0.0)
    st = lambda a, b: jnp.stack([a, b]).astype(F32)
    return st(inv0, inv1), st(mc0, mc1), st(m10, m11), st(ma0, ma1), st(mb0, mb1)


def _rope_call(positions):
    B, S = positions.shape
    ts = min(S, 512)
    consts = _rope_consts()
    cspec = pl.BlockSpec((2, 128), lambda b, i: (0, 0))
    return pl.pallas_call(
        _rope_kernel,
        grid=(B, S // ts),
        in_specs=[pl.BlockSpec((1, ts, 1), lambda b, i: (b, i, 0))] + [cspec] * 5,
        out_specs=[pl.BlockSpec((2, 1, ts, 384), lambda b, i: (0, b, i, 0)),
                   pl.BlockSpec((1, 384, ts), lambda b, i: (b, 0, i))],
        out_shape=[jax.ShapeDtypeStruct((2, B, S, 384), F32), jax.ShapeDtypeStruct((B, 384, S), F32)],
        compiler_params=_cparams("parallel", "parallel"),
        name="rope_tables",
    )(positions.reshape(B, S, 1), *consts)


def _apply_rope(x, tab, shift):
    n = x.shape[1] // 128
    width = x.shape[1]
    cc = jnp.tile(tab[:, 0:128], (1, n))
    sa = jnp.tile(tab[:, 128:256], (1, n))
    sb = jnp.tile(tab[:, 256:384], (1, n))
    return x * cc + pltpu.roll(x, width - shift, axis=1) * sa + pltpu.roll(x, shift, axis=1) * sb


def _ffn_kernel(x_ref, sh_ref, sc_ref, g_ref, wi_ref, wo_ref, fg_ref, o_ref, *, final):
    x = x_ref[0]
    h = _rms_mod(x, sh_ref[0], sc_ref[0]).astype(BF16)
    acc = None
    for c in range(D_FF // FFN_TF):
        cols = slice(c * FFN_TF, (c + 1) * FFN_TF)
        gate = _dot(h, wi_ref[:, cols])
        up = _dot(h, wi_ref[:, D_FF + c * FFN_TF:D_FF + (c + 1) * FFN_TF])
        act = (gate * _sigmoid(gate) * up).astype(BF16)
        part = _dot(act, wo_ref[cols, :])
        acc = part if acc is None else acc + part
    y = x + (0.5 * g_ref[0]) * acc
    if final:
        ms = jnp.mean(y * y, axis=-1, keepdims=True)
        y = (y * lax.rsqrt(ms + NORM_EPS)) * fg_ref[...]
    o_ref[0] = y


def _ffn_call(x, sh, sc, g, w_in_bf, w_out_bf, layer, final_g, final):
    B, S, D = x.shape
    tm = min(S, FFN_TM)
    vec = pl.BlockSpec((1, 1, D), lambda b, i: (b, 0, 0))
    resident = lambda shape: pl.BlockSpec((None,) + shape, lambda b, i: (layer, 0, 0), pipeline_mode=pl.Buffered(1))
    return pl.pallas_call(
        functools.partial(_ffn_kernel, final=final),
        grid=(B, S // tm),
        in_specs=[
            pl.BlockSpec((1, tm, D), lambda b, i: (b, i, 0)),
            vec, vec, vec,
            resident((D, 2 * D_FF)), resident((D_FF, D)),
            pl.BlockSpec((1, D), lambda b, i: (0, 0)),
        ],
        out_specs=pl.BlockSpec((1, tm, D), lambda b, i: (b, i, 0)),
        out_shape=jax.ShapeDtypeStruct((B, S, D), F32),
        compiler_params=_cparams("parallel", "parallel"),
        name="ffn",
    )(x, sh, sc, g, w_in_bf, w_out_bf, final_g)


ZB_COLS = MLA_Q_RANK + MLA_KV_RANK + MLA_HEAD_PAD
ZC_COLS = 4 * RET_WIDTH
Z_COLS = RWKV_COLS + ZB_COLS + ZC_COLS


def _inproj_kernel(x_ref, sh_ref, sc_ref, w_ref, tab_ref, za_ref, zb_ref, zc_ref):
    h = _rms_mod(x_ref[0], sh_ref[0], sc_ref[0]).astype(BF16)
    z = _dot(h, w_ref[...])
    za_ref[0] = z[:, 0:RWKV_COLS]
    zb_ref[0] = z[:, RWKV_COLS:RWKV_COLS + ZB_COLS].astype(BF16)
    o = RWKV_COLS + ZB_COLS
    tab = tab_ref[0, 0]
    zc_ref[0, :, 0:256] = _apply_rope(z[:, o:o + 256], tab, 32).astype(BF16)
    zc_ref[0, :, 256:512] = (_apply_rope(z[:, o + 256:o + 512], tab, 32) * (HEAD_DIM ** -0.5)).astype(BF16)
    zc_ref[0, :, 512:1024] = z[:, o + 512:Z_COLS].astype(BF16)


def _inproj_call(x, sh, sc, wz_bf, tabs):
    B, S, D = x.shape
    tm = min(S, 512)
    vec = pl.BlockSpec((1, 1, D), lambda b, i: (b, 0, 0))
    out = lambda n: pl.BlockSpec((1, tm, n), lambda b, i: (b, i, 0))
    return pl.pallas_call(
        _inproj_kernel,
        grid=(B, S // tm),
        in_specs=[
            pl.BlockSpec((1, tm, D), lambda b, i: (b, i, 0)),
            vec, vec,
            pl.BlockSpec((D, Z_COLS), lambda b, i: (0, 0)),
            pl.BlockSpec((1, 1, tm, 384), lambda b, i: (1, b, i, 0)),
        ],
        out_specs=[out(RWKV_COLS), out(ZB_COLS), out(ZC_COLS)],
        out_shape=[jax.ShapeDtypeStruct((B, S, n), dt)
                   for n, dt in ((RWKV_COLS, F32), (ZB_COLS, BF16), (ZC_COLS, BF16))],
        compiler_params=_cparams("parallel", "parallel"),
        name="inproj",
    )(x, sh, sc, wz_bf, tabs)


def _rwkv_prep_kernel(z_ref, zp_ref, zn_ref, mu_ref, w0_ref, wup_ref, a0_ref, aup_ref, gup_ref,
                      kk_ref, ka_ref, rk_ref, ones_ref, pc_ref, pd_ref, pw_ref, pg_ref, *, ts, nt):
    i = pl.program_id(1)
    z = z_ref[0]
    row = lax.broadcasted_iota(jnp.int32, (ts, 1), 0)
    prev_edge = jnp.where(i > 0, zp_ref[0, 7:8, :], 0.0)
    next_edge = jnp.where(i < nt - 1, zn_ref[0, 0:1, :], 0.0)
    prev = jnp.where(row == 0, prev_edge, pltpu.roll(z, 1, axis=0))
    nxt = jnp.where(row == ts - 1, next_edge, pltpu.roll(z, ts - 1, axis=0))
    zs = z + mu_ref[...] * (0.5 * (prev + nxt) - z)

    r = zs[:, 0:256]
    k = zs[:, 256:512]
    v = zs[:, 512:768]
    g_lo = zs[:, 768:896]
    w_lo = zs[:, 896:1024]
    a_lo = zs[:, 1024:1152]

    w_raw = w0_ref[...] + _dg3(jnp.tanh(w_lo), wup_ref[...], _NN)
    logw = (-math.exp(-0.5)) * _sigmoid(w_raw)
    a = _sigmoid(a0_ref[...] + _dg3(a_lo, aup_ref[...], _NN))
    g = _dg3(_sigmoid(g_lo), gup_ref[...], _NN)

    ones_bd = ones_ref[...]
    kk0 = k * kk_ref[...]
    nrm = jnp.sqrt(_head_sums(kk0 * kk0, ones_bd))
    kk = kk0 / jnp.maximum(nrm, 1e-12)
    bonus = _head_sums(r * k * rk_ref[...], ones_bd) * v

    pc_ref[0, :, 0:256] = r.astype(BF16)
    pc_ref[0, :, 256:512] = v.astype(BF16)
    pc_ref[0, :, 512:768] = kk.astype(BF16)
    for d in range(2):
        a_d = a[:, 256 * d:256 * d + 256]
        pd_ref[d, 0, :, 0:256] = (k * (1.0 + (a_d - 1.0) * ka_ref[...])).astype(BF16)
        pd_ref[d, 0, :, 256:512] = (kk * a_d).astype(BF16)
        pw_ref[d, 0] = logw[:, 256 * d:256 * d + 256]
    pg_ref[0, :, 0:256] = g.astype(BF16)
    pg_ref[0, :, 256:512] = bonus.astype(BF16)


def _rwkv_prep_call(za, mu, w0, wup_bd, a0, aup_bd, gup, k_k, k_a, r_k, ones_bd):
    B, S, _ = za.shape
    ts = min(S, 512)
    nt = S // ts
    nb8 = S // 8
    const = lambda shape: pl.BlockSpec(shape, lambda b, i: (0,) * len(shape))
    return pl.pallas_call(
        functools.partial(_rwkv_prep_kernel, ts=ts, nt=nt),
        grid=(B, nt),
        in_specs=[
            pl.BlockSpec((1, ts, RWKV_COLS), lambda b, i: (b, i, 0)),
            pl.BlockSpec((1, 8, RWKV_COLS), lambda b, i: (b, jnp.maximum(i * (ts // 8) - 1, 0), 0)),
            pl.BlockSpec((1, 8, RWKV_COLS), lambda b, i: (b, jnp.minimum((i + 1) * (ts // 8), nb8 - 1), 0)),
            const((1, RWKV_COLS)),
            const((1, 512)), const((128, 512)), const((1, 512)), const((128, 512)), const((128, 256)),
            const((1, 256)), const((1, 256)), const((1, 256)), const((256, 256)),
        ],
        out_specs=[
            pl.BlockSpec((1, ts, 768), lambda b, i: (b, i, 0)),
            pl.BlockSpec((2, 1, ts, 512), lambda b, i: (0, b, i, 0)),
            pl.BlockSpec((2, 1, ts, 256), lambda b, i: (0, b, i, 0)),
            pl.BlockSpec((1, ts, 512), lambda b, i: (b, i, 0)),
        ],
        out_shape=[
            jax.ShapeDtypeStruct((B, S, 768), BF16),
            jax.ShapeDtypeStruct((2, B, S, 512), BF16),
            jax.ShapeDtypeStruct((2, B, S, 256), F32),
            jax.ShapeDtypeStruct((B, S, 512), BF16),
        ],
        compiler_params=_cparams("parallel", "parallel"),
        name="rwkv_prep",
    )(za, za, za, mu, w0, wup_bd, a0, aup_bd, gup, k_k, k_a, r_k, ones_bd)


def _rwkv_dir_operands(direction, pc, pd, logw):
    C = RWKV_CHUNK
    r = pc[:, 0:256]
    v = pc[:, 256:512]
    kk = pc[:, 512:768]
    kmod = pd[:, 0:256]
    kka = pd[:, 256:512]

    row = lax.broadcasted_iota(jnp.int32, (C, C), 0)
    col = lax.broadcasted_iota(jnp.int32, (C, C), 1)
    tri = jnp.where((col <= row) if direction == 0 else (col >= row), 1.0, 0.0).astype(BF16)
    lw_hi, lw_lo = _split(logw)
    l_in = _dot(tri, lw_hi) + _dot(tri, lw_lo)
    l_ex = l_in - logw
    l_tot = l_in[C - 1:C, :] if direction == 0 else l_in[0:1, :]
    e_ex = jnp.exp(l_ex)
    e_neg = jnp.exp(-l_in)
    e_tot = jnp.exp(l_tot - l_in)
    kkg = kk * e_ex
    bbar = kka * e_neg
    kbar = kmod * e_neg
    bg = kka * e_tot
    kg = kmod * e_tot
    rg = r * jnp.exp(l_in) if direction == 0 else r * e_ex
    g_tot = jnp.exp(l_tot)
    return dict(kkg=kkg, bbar=bbar, kbar=kbar, bg=bg, kg=kg, rg=rg, v=v, g_tot=g_tot)


def _operand(a, passes):
    hi = a.astype(BF16)
    lo = (a - hi.astype(F32)).astype(BF16) if passes > 1 else None
    return hi, lo


def _mm(a, b, dims):
    dn = (dims, ((), ()))
    out = lax.dot_general(a[0], b[0], dn, preferred_element_type=F32)
    if b[1] is not None:
        out = out + lax.dot_general(a[0], b[1], dn, preferred_element_type=F32)
    if a[1] is not None:
        out = out + lax.dot_general(a[1], b[0], dn, preferred_element_type=F32)
    return out


def _rwkv_scan_kernel(pcf_ref, pcb_ref, pdf_ref, pdb_ref, pwf_ref, pwb_ref, yf_ref, yb_ref, s_ref):
    C = RWKV_CHUNK
    P = RWKV_PASSES

    @pl.when(pl.program_id(1) == 0)
    def _():
        s_ref[...] = jnp.zeros_like(s_ref)

    row = lax.broadcasted_iota(jnp.int32, (C, C), 0)
    col = lax.broadcasted_iota(jnp.int32, (C, C), 1)
    eye = col == row
    ident = jnp.where(eye, 1.0, 0.0)
    nb = pcf_ref.shape[0]
    ops = {}
    for bb in range(nb):
        ops[bb, 0] = _rwkv_dir_operands(0, pcf_ref[bb].astype(F32), pdf_ref[0, bb].astype(F32), pwf_ref[0, bb])
        ops[bb, 1] = _rwkv_dir_operands(1, pcb_ref[bb].astype(F32), pdb_ref[0, bb].astype(F32), pwb_ref[0, bb])
    earlier = (col < row, col > row)
    ymask = (col <= row, col > row)
    units = [(bb, d, h) for bb in range(nb) for d in range(2) for h in range(RWKV_HEADS)]
    nu = len(units)
    sl = lambda h: slice(64 * h, 64 * h + 64)
    s0 = [s_ref[u] for u in units]

    def head(name):
        return [_operand(ops[bb, d][name][:, sl(h)], P) for bb, d, h in units]

    kkg, bbar, kbar, bg, kg, rg, v = (head(n) for n in ("kkg", "bbar", "kbar", "bg", "kg", "rg", "v"))
    akk = [jnp.where(earlier[d], _mm(kkg[i], bbar[i], _NT), 0.0) for i, (_, d, _) in enumerate(units)]
    bk = [jnp.where(earlier[d], _mm(kkg[i], kbar[i], _NT), 0.0) for i, (_, d, _) in enumerate(units)]
    ark = [jnp.where(ymask[d], _mm(rg[i], bbar[i], _NT), 0.0) for i, (_, d, _) in enumerate(units)]
    brk = [jnp.where(ymask[d], _mm(rg[i], kbar[i], _NT), 0.0) for i, (_, d, _) in enumerate(units)]
    pw = [-a for a in akk]
    t_inv = [ident + x for x in pw]
    for _ in range(int(math.log2(C)) - 1):
        pw_o = [_operand(x, P) for x in pw]
        pw = [_mm(x, x, _NN) for x in pw_o]
        pw_o = [_operand(x, P) for x in pw]
        t_inv = [t + _mm(_operand(t, P), x, _NN) for t, x in zip(t_inv, pw_o)]
    t_o = [_operand(t, P) for t in t_inv]
    bkv = [_mm(_operand(b, P), x, _NN) for b, x in zip(bk, v)]
    w_m = [_mm(t, x, _NN) for t, x in zip(t_o, kkg)]
    u_m = [_mm(t, _operand(x, P), _NN) for t, x in zip(t_o, bkv)]
    w_o = [_operand(x, P) for x in w_m]
    u_o = [_operand(x, P) for x in u_m]
    ark_o = [_operand(x, P) for x in ark]
    brk_o = [_operand(x, P) for x in brk]
    m_m = [jnp.where(eye, ops[bb, d]["g_tot"][:, sl(h)], 0.0) - _mm(w_o[i], bg[i], _TN)
           for i, (bb, d, h) in enumerate(units)]
    n_m = [_mm(v[i], kg[i], _TN) - _mm(u_o[i], bg[i], _TN) for i in range(nu)]
    q_m = [ops[bb, d]["rg"][:, sl(h)] - _mm(ark_o[i], w_o[i], _NN) for i, (bb, d, h) in enumerate(units)]
    y_i = [_mm(brk_o[i], v[i], _NN) - _mm(ark_o[i], u_o[i], _NN) for i in range(nu)]
    s_o = [_operand(s, P) for s in s0]
    y = [y_i[i] + _mm(_operand(q_m[i], P), s_o[i], _NT) for i in range(nu)]
    s_new = [_mm(s_o[i], _operand(m_m[i], P), _NN) + n_m[i] for i in range(nu)]
    for i, (bb, d, h) in enumerate(units):
        (yf_ref if d == 0 else yb_ref)[bb, :, sl(h)] = y[i]
        s_ref[bb, d, h] = s_new[i]


def _rwkv_scan_call(pc, pd, pw):
    B, S, _ = pc.shape
    C = RWKV_CHUNK
    nc = S // C
    nb = math.gcd(B, RWKV_BATCH_PER_STEP)
    return pl.pallas_call(
        _rwkv_scan_kernel,
        grid=(B // nb, nc),
        in_specs=[
            pl.BlockSpec((nb, C, 768), lambda b, c: (b, c, 0)),
            pl.BlockSpec((nb, C, 768), lambda b, c: (b, nc - 1 - c, 0)),
            pl.BlockSpec((1, nb, C, 512), lambda b, c: (0, b, c, 0)),
            pl.BlockSpec((1, nb, C, 512), lambda b, c: (1, b, nc - 1 - c, 0)),
            pl.BlockSpec((1, nb, C, 256), lambda b, c: (0, b, c, 0)),
            pl.BlockSpec((1, nb, C, 256), lambda b, c: (1, b, nc - 1 - c, 0)),
        ],
        out_specs=[
            pl.BlockSpec((nb, C, 256), lambda b, c: (b, c, 0)),
            pl.BlockSpec((nb, C, 256), lambda b, c: (b, nc - 1 - c, 0)),
        ],
        out_shape=[jax.ShapeDtypeStruct((B, S, 256), F32)] * 2,
        scratch_shapes=[pltpu.VMEM((nb, 2, RWKV_HEADS, HEAD_DIM, HEAD_DIM), F32)],
        compiler_params=_cparams("parallel", "arbitrary"),
        name="rwkv_scan",
    )(pc, pc, pd, pd, pw, pw)


def _mla_prep_kernel(zb_ref, tab_ref, tabt_ref, qg_ref, wqt_ref, kg_ref, wk_ref, wvt_ref, hsel_ref,
                     qt_ref, k_ref, vt_ref, kn_ref):
    zb = zb_ref[0].astype(F32)
    tab = tab_ref[0, 0]
    cq = zb[:, 0:MLA_Q_RANK]
    ckv = zb[:, MLA_Q_RANK:MLA_Q_RANK + MLA_KV_RANK]
    kpe = zb[:, MLA_Q_RANK + MLA_KV_RANK:ZB_COLS]

    def rms(t):
        return t * lax.rsqrt(jnp.mean(t * t, axis=-1, keepdims=True) + NORM_EPS)

    cqn = (rms(cq) * qg_ref[...]).astype(BF16)
    ckvn = (rms(ckv) * kg_ref[...]).astype(BF16)
    nt_dims = (_NT, ((), ()))
    qt = lax.dot_general(wqt_ref[...], cqn, nt_dims, preferred_element_type=F32)
    rows = qt.shape[0]
    tabt = tabt_ref[0]
    cc, sa, sb = (jnp.tile(tabt[128 * j:128 * j + 128], (MLA_HEADS, 1)) for j in range(3))
    qt = qt * cc + pltpu.roll(qt, rows - 16, axis=0) * sa + pltpu.roll(qt, 16, axis=0) * sb
    k = _dot(ckvn, wk_ref[...])
    kpe = _apply_rope(kpe, tab, 16)
    k = k + jnp.tile(kpe, (1, MLA_HEADS))
    qt_ref[0] = (qt * (math.log2(math.e) * (MLA_NOPE + MLA_ROPE) ** -0.5)).astype(BF16)
    kb = k.astype(BF16)
    k_ref[0] = kb
    kf = kb.astype(F32)
    k_sq = jnp.max(_head_sums(kf * kf, hsel_ref[...]), axis=0, keepdims=True)
    kn_ref[0, 0] = jnp.broadcast_to(k_sq, (8, 128))
    vt_ref[0, 0] = lax.dot_general(wvt_ref[...], ckvn, nt_dims, preferred_element_type=F32).astype(BF16)


def _mla_key_tile(S):
    return min(S, MLA_TK)


def _mla_prep_call(zb, tabs, tab_t, qg, wqt_bf, kg, wk_bf, wvt_bf, head_sel):
    B, S, _ = zb.shape
    ts = _mla_key_tile(S)
    HP = MLA_HEADS * MLA_HEAD_PAD
    const = lambda shape: pl.BlockSpec(shape, lambda b, i: (0,) * len(shape))
    return pl.pallas_call(
        _mla_prep_kernel,
        grid=(B, S // ts),
        in_specs=[
            pl.BlockSpec((1, ts, ZB_COLS), lambda b, i: (b, i, 0)),
            pl.BlockSpec((1, 1, ts, 384), lambda b, i: (0, b, i, 0)),
            pl.BlockSpec((1, 384, ts), lambda b, i: (b, 0, i)),
            const((1, MLA_Q_RANK)), const((HP, MLA_Q_RANK)),
            const((1, MLA_KV_RANK)), const((MLA_KV_RANK, HP)), const((MLA_WIDTH, MLA_KV_RANK)),
            const((HP, 128)),
        ],
        out_specs=[
            pl.BlockSpec((1, HP, ts), lambda b, i: (b, 0, i)),
            pl.BlockSpec((1, ts, HP), lambda b, i: (b, i, 0)),
            pl.BlockSpec((1, 1, MLA_WIDTH, ts), lambda b, i: (b, i, 0, 0)),
            pl.BlockSpec((1, 1, 8, 128), lambda b, i: (b, i, 0, 0)),
        ],
        out_shape=[
            jax.ShapeDtypeStruct((B, HP, S), BF16),
            jax.ShapeDtypeStruct((B, S, HP), BF16),
            jax.ShapeDtypeStruct((B, S // ts, MLA_WIDTH, ts), BF16),
            jax.ShapeDtypeStruct((B, S // ts, 8, 128), F32),
        ],
        compiler_params=_cparams("parallel", "parallel"),
        name="mla_prep",
    )(zb, tabs, tab_t, qg, wqt_bf, kg, wk_bf, wvt_bf, head_sel)


def _mla_attn_kernel(qt_ref, k_ref, vt_ref, kn_ref, o_ref, *, tk, nk):
    tq = qt_ref.shape[2]
    cw = min(tq, MLA_QCOLS)
    ncol = tq // cw
    chunks = [(hh, c) for hh in range(2) for c in range(ncol)]
    ones = jnp.ones((MLA_ONES_ROWS, tk), BF16)
    acc0 = jnp.zeros((MLA_V + MLA_ONES_ROWS, cw), F32)

    def scores(i):
        ks = pl.multiple_of(i * tk, tk)
        return [_dot(k_ref[0, pl.ds(ks, tk), 128 * hh:128 * hh + 128],
                     qt_ref[0, 128 * hh:128 * hh + 128, c * cw:(c + 1) * cw]) for hh, c in chunks]

    def values(i, hh):
        return jnp.concatenate([vt_ref[0, i, 64 * hh:64 * hh + 64, :], ones], axis=0)

    k_sq = jnp.max(kn_ref[0], axis=(0, 1), keepdims=True)[0]
    lane = lax.broadcasted_iota(jnp.int32, k_sq.shape, 1)
    head0 = 2 * pl.program_id(1)
    bounds = []
    for hh, c in chunks:
        k_h = jnp.sum(jnp.where(lane == head0 + hh, k_sq, 0.0), axis=1, keepdims=True)
        qf = qt_ref[0, 128 * hh:128 * hh + 128, c * cw:(c + 1) * cw].astype(F32)
        bounds.append(jnp.sqrt(jnp.sum(qf * qf, axis=0, keepdims=True) * k_h) * MLA_BOUND_MARGIN)
    worst = functools.reduce(jnp.maximum, [jnp.max(b) for b in bounds])

    def bounded_offset():
        def body(i, carry):
            ps = [jnp.exp2(s - bounds[j]).astype(BF16) for j, s in enumerate(scores(i))]
            return tuple(carry[j] + _dot(values(i, hh), ps[j]) for j, (hh, c) in enumerate(chunks))

        return lax.fori_loop(0, nk, body, (acc0,) * len(chunks), unroll=math.gcd(nk, MLA_BOUNDED_UNROLL))

    def running_max():
        def body(i, carry):
            stats = []
            for j, s in enumerate(scores(i)):
                m = carry[j][0]
                m_new = jnp.maximum(m, jnp.max(s, axis=0, keepdims=True))
                stats.append((m_new, jnp.exp2(m - m_new), jnp.exp2(s - m_new).astype(BF16)))
            new = []
            for j, (hh, c) in enumerate(chunks):
                m_new, alpha, p = stats[j]
                new.append((m_new, alpha * carry[j][1] + _dot(values(i, hh), p)))
            return tuple(new)

        init = (jnp.full((1, cw), -jnp.inf, F32), acc0)
        return tuple(r[1] for r in lax.fori_loop(0, nk, body, (init,) * len(chunks)))

    res = lax.cond(worst <= MLA_BOUND_LIMIT, bounded_offset, running_max)
    out_t = [jnp.concatenate([res[hh * ncol + c][0:MLA_V] / res[hh * ncol + c][MLA_V:MLA_V + 1]
                              for c in range(ncol)], axis=1) for hh in range(2)]
    o_ref[0] = jnp.concatenate(out_t, axis=0).T.astype(BF16)


def _mla_attn_call(qt, k, vt, kn):
    B, S, _ = k.shape
    tq = min(S, MLA_TQ)
    tk = _mla_key_tile(S)
    nk = S // tk
    return pl.pallas_call(
        functools.partial(_mla_attn_kernel, tk=tk, nk=nk),
        grid=(B, MLA_HEADS // 2, S // tq),
        in_specs=[
            pl.BlockSpec((1, 256, tq), lambda b, h, i: (b, h, i)),
            pl.BlockSpec((1, S, 256), lambda b, h, i: (b, 0, h)),
            pl.BlockSpec((1, nk, 128, tk), lambda b, h, i: (b, 0, h, 0)),
            pl.BlockSpec((1, nk, 8, 128), lambda b, h, i: (b, 0, 0, 0)),
        ],
        out_specs=pl.BlockSpec((1, tq, 128), lambda b, h, i: (b, i, h)),
        out_shape=jax.ShapeDtypeStruct((B, S, MLA_WIDTH), BF16),
        compiler_params=_cparams("parallel", "parallel", "arbitrary"),
        name="mla_attn",
    )(qt, k, vt, kn)


def _ret_dir_operands(direction, zc, lrv, lrh):
    C = RET_CHUNK
    q = zc[:, 0:256]
    k = zc[:, 256:512]
    v = zc[:, 512:768]
    lgv = -jnp.exp(lrv)
    lgh = -jnp.exp(lrh)
    pos = lax.broadcasted_iota(jnp.int32, (C, 1), 0).astype(F32)
    row = lax.broadcasted_iota(jnp.int32, (C, C), 0)
    col = lax.broadcasted_iota(jnp.int32, (C, C), 1)
    if direction == 0:
        kw = k * jnp.exp(lgv * (C - 1.0 - pos))
        qw = q * jnp.exp(lgv * (pos + 1.0))
        mask = col <= row
        dist = (row - col).astype(F32)
    else:
        kw = k * jnp.exp(lgv * pos)
        qw = q * jnp.exp(lgv * (C - pos))
        mask = col > row
        dist = (col - row).astype(F32)
    dist = jnp.maximum(dist, 0.0)
    dmats = [jnp.where(mask, jnp.exp(lgh[h:h + 1, :] * dist), 0.0) for h in range(RET_HEADS)]
    decays = [jnp.exp(lgh[h:h + 1, 0:64] * C) for h in range(RET_HEADS)]
    qb, kb, vb, kwb, qwb = (t.astype(BF16) for t in (q, k, v, kw, qw))
    return dict(q=qb, k=kb, v=vb, kw=kwb, qw=qwb, dmat=dmats, decay=decays)


def _ret_kernel(zf_ref, zb_ref, lrv_ref, lrh_ref, yf_ref, yb_ref, r_ref):
    @pl.when(pl.program_id(1) == 0)
    def _():
        r_ref[...] = jnp.zeros_like(r_ref)

    nb = zf_ref.shape[0]
    ops = {}
    for bb in range(nb):
        ops[bb, 0] = _ret_dir_operands(0, zf_ref[bb].astype(F32), lrv_ref[0:1, :], lrh_ref[0:4, :])
        ops[bb, 1] = _ret_dir_operands(1, zb_ref[bb].astype(F32), lrv_ref[1:2, :], lrh_ref[4:8, :])
    units = [(bb, d, h) for bb in range(nb) for d in range(2) for h in range(RET_HEADS)]
    sl = lambda h: slice(64 * h, 64 * h + 64)
    r0 = [r_ref[u] for u in units]
    sc = [lax.dot_general(ops[bb, d]["q"][:, sl(h)], ops[bb, d]["k"][:, sl(h)], (_NT, ((), ())),
                          preferred_element_type=F32) * ops[bb, d]["dmat"][h] for bb, d, h in units]
    kv = [lax.dot_general(ops[bb, d]["kw"][:, sl(h)], ops[bb, d]["v"][:, sl(h)], (_TN, ((), ())),
                          preferred_element_type=F32) for bb, d, h in units]
    cross = [_dot(ops[bb, d]["qw"][:, sl(h)], r0[i].astype(BF16)) for i, (bb, d, h) in enumerate(units)]
    inner = [_dot(sc[i].astype(BF16), ops[bb, d]["v"][:, sl(h)]) for i, (bb, d, h) in enumerate(units)]
    for i, (bb, d, h) in enumerate(units):
        (yf_ref if d == 0 else yb_ref)[bb, :, sl(h)] = inner[i] + cross[i]
        r_ref[bb, d, h] = r0[i] * ops[bb, d]["decay"][h] + kv[i]


def _ret_call(zc, lr_vec, lr_heads):
    B, S, _ = zc.shape
    C = RET_CHUNK
    nc = S // C
    nb = math.gcd(B, RET_BATCH_PER_STEP)
    const = lambda shape: pl.BlockSpec(shape, lambda b, c: (0,) * len(shape))
    return pl.pallas_call(
        _ret_kernel,
        grid=(B // nb, nc),
        in_specs=[
            pl.BlockSpec((nb, C, ZC_COLS), lambda b, c: (b, c, 0)),
            pl.BlockSpec((nb, C, ZC_COLS), lambda b, c: (b, nc - 1 - c, 0)),
            const((2, 256)), const((8, 128)),
        ],
        out_specs=[
            pl.BlockSpec((nb, C, 256), lambda b, c: (b, c, 0)),
            pl.BlockSpec((nb, C, 256), lambda b, c: (b, nc - 1 - c, 0)),
        ],
        out_shape=[jax.ShapeDtypeStruct((B, S, 256), F32)] * 2,
        scratch_shapes=[pltpu.VMEM((nb, 2, RET_HEADS, HEAD_DIM, HEAD_DIM), F32)],
        compiler_params=_cparams("parallel", "arbitrary"),
        name="retention",
    )(zc, zc, lr_vec, lr_heads)


def _head_norm(y, avg_bd, eps):
    mu = _head_sums(y, avg_bd)
    d = y - mu
    var = _head_sums(d * d, avg_bd)
    return d * lax.rsqrt(var + eps)


def _outproj_kernel(x_ref, g2_ref, yf_ref, yb_ref, pg_ref, lng_ref, lnb_ref, om_ref, rf_ref, rb_ref,
                    gate_ref, gng_ref, avg_ref, w_ref, o_ref):
    avg_bd = avg_ref[...]
    y = _head_norm(yf_ref[0] + yb_ref[0], avg_bd, RWKV_LN_EPS) * lng_ref[...] + lnb_ref[...]
    pg = pg_ref[0].astype(F32)
    o_a = (y + pg[:, 256:512]) * pg[:, 0:256]
    yr = _head_norm(rf_ref[0] + rb_ref[0], avg_bd, RET_LN_EPS) * gng_ref[...]
    gate = gate_ref[0].astype(F32)
    o_c = (gate * _sigmoid(gate)) * yr
    mixed = _dot(o_a.astype(BF16), w_ref[0:RWKV_WIDTH, :])
    mixed = mixed + _dot(om_ref[0], w_ref[RWKV_WIDTH:RWKV_WIDTH + MLA_WIDTH, :])
    mixed = mixed + _dot(o_c.astype(BF16), w_ref[RWKV_WIDTH + MLA_WIDTH:D_MODEL, :])
    o_ref[0] = x_ref[0] + g2_ref[0] * mixed


def _outproj_call(x, g2, yf, yb, pg, ln_g, ln_b, o_mla, rf, rb, zc, gn_g, avg_bd, w_out_bf, layer):
    B, S, D = x.shape
    tm = min(S, 512)
    tok = lambda n: pl.BlockSpec((1, tm, n), lambda b, i: (b, i, 0))
    const = lambda shape: pl.BlockSpec(shape, lambda b, i: (0,) * len(shape))
    return pl.pallas_call(
        _outproj_kernel,
        grid=(B, S // tm),
        in_specs=[
            tok(D),
            pl.BlockSpec((1, 1, D), lambda b, i: (b, 0, 0)),
            tok(256), tok(256), tok(512), const((1, 256)), const((1, 256)),
            tok(512), tok(256), tok(256),
            pl.BlockSpec((1, tm, 256), lambda b, i: (b, i, 3)),
            const((1, 256)), const((256, 256)),
            pl.BlockSpec((None, D, D), lambda b, i: (layer, 0, 0)),
        ],
        out_specs=tok(D),
        out_shape=jax.ShapeDtypeStruct((B, S, D), F32),
        compiler_params=_cparams("parallel", "parallel"),
        name="outproj",
    )(x, g2, yf, yb, pg, ln_g, ln_b, o_mla, rf, rb, zc, gn_g, avg_bd, w_out_bf)


def _block_diag2(w):
    z = jnp.zeros_like(w[0])
    return jnp.concatenate([jnp.concatenate([w[0], z], axis=1), jnp.concatenate([z, w[1]], axis=1)], axis=0)


def _pad_inproj(w):
    D = w.shape[0]
    o = RWKV_COLS + MLA_Q_RANK + MLA_KV_RANK
    return jnp.concatenate(
        [w[:, :o], jnp.zeros((D, 64), w.dtype), w[:, o:o + MLA_ROPE], jnp.zeros((D, 32), w.dtype), w[:, o + MLA_ROPE:]],
        axis=1)


def _pad_wq(w):
    r = w.shape[0]
    w = w.reshape(r, MLA_HEADS, MLA_NOPE + MLA_ROPE)
    w = jnp.concatenate([w, jnp.zeros((r, MLA_HEADS, 32), w.dtype)], axis=2)
    return w.reshape(r, MLA_HEADS * MLA_HEAD_PAD)


def _split_wkv(w):
    r = w.shape[0]
    w = w.reshape(r, MLA_HEADS, MLA_NOPE + MLA_V)
    wk = jnp.concatenate([w[:, :, :MLA_NOPE], jnp.zeros((r, MLA_HEADS, 64), w.dtype)], axis=2)
    return wk.reshape(r, MLA_HEADS * MLA_HEAD_PAD), w[:, :, MLA_NOPE:].reshape(r, MLA_WIDTH)


def kernel(x, c, positions, w_ada, b_ada, w_ff1_in, w_ff1_out, w_ff2_in, w_ff2_out, w_in, w_out, rwkv_mu, rwkv_w0,
           rwkv_w_up, rwkv_a0, rwkv_a_up, rwkv_g_up, rwkv_k_k, rwkv_k_a, rwkv_r_k, rwkv_ln_g, rwkv_ln_b,
           mla_q_norm_g, mla_w_uq, mla_kv_norm_g, mla_w_ukv, ret_log_rate, ret_gn_g, final_norm_g):
    B, S, D = x.shape
    L = w_ada.shape[0]
    mod = _mod_call(c, w_ada, b_ada)
    tabs, tab_t = _rope_call(positions)

    head_id = jnp.arange(256) // HEAD_DIM
    same_head = (head_id[:, None] == head_id[None, :]).astype(F32)
    ones_bd = same_head
    avg_bd = same_head / HEAD_DIM
    final_g = final_norm_g.reshape(1, D)
    w_ff1_in_bf, w_ff1_out_bf = w_ff1_in.astype(BF16), w_ff1_out.astype(BF16)
    w_ff2_in_bf, w_ff2_out_bf = w_ff2_in.astype(BF16), w_ff2_out.astype(BF16)
    w_out_bf = w_out.astype(BF16)
    head_sel = (jnp.arange(MLA_HEADS * MLA_HEAD_PAD)[:, None] // MLA_HEAD_PAD == jnp.arange(128)[None, :]).astype(F32)

    for l in range(L):
        m = [mod[l, :, i * D:(i + 1) * D].reshape(B, 1, D) for i in range(N_MOD)]
        sh1, sc1, g1, sh2, sc2, g2, sh3, sc3, g3 = m

        x = _ffn_call(x, sh1, sc1, g1, w_ff1_in_bf, w_ff1_out_bf, l, final_g, False)

        za, zb, zc = _inproj_call(x, sh2, sc2, _pad_inproj(w_in[l]).astype(BF16), tabs)

        pc, pd, pw, pg = _rwkv_prep_call(
            za, rwkv_mu[l].reshape(1, -1), rwkv_w0[l].reshape(1, 512), _block_diag2(rwkv_w_up[l]),
            rwkv_a0[l].reshape(1, 512), _block_diag2(rwkv_a_up[l]), rwkv_g_up[l],
            rwkv_k_k[l].reshape(1, 256), rwkv_k_a[l].reshape(1, 256), rwkv_r_k[l].reshape(1, 256), ones_bd)
        yf, yb = _rwkv_scan_call(pc, pd, pw)

        wk, wv = _split_wkv(mla_w_ukv[l])
        qt, k, vt, kn = _mla_prep_call(zb, tabs, tab_t, mla_q_norm_g[l].reshape(1, -1),
                                       _pad_wq(mla_w_uq[l]).T.astype(BF16), mla_kv_norm_g[l].reshape(1, -1),
                                       wk.astype(BF16), wv.T.astype(BF16), head_sel)
        o_mla = _mla_attn_call(qt, k, vt, kn)

        lr = ret_log_rate[l]
        lr_vec = jnp.repeat(lr, HEAD_DIM, axis=1)
        lr_heads = jnp.broadcast_to(lr.reshape(8, 1), (8, 128))
        rf, rb = _ret_call(zc, lr_vec, lr_heads)

        x = _outproj_call(x, g2, yf, yb, pg, rwkv_ln_g[l].reshape(1, 256), rwkv_ln_b[l].reshape(1, 256), o_mla,
                          rf, rb, zc, ret_gn_g[l].reshape(1, 256), avg_bd, w_out_bf, l)

        x = _ffn_call(x, sh3, sc3, g3, w_ff2_in_bf, w_ff2_out_bf, l, final_g, l == L - 1)
    return x
```

```python
import functools
import math

import jax
import jax.numpy as jnp
from jax import lax
from jax.experimental import pallas as pl
from jax.experimental.pallas import tpu as pltpu

F32 = jnp.float32
BF16 = jnp.bfloat16

D_MODEL = 1024
HEAD_DIM = 64
RWKV_WIDTH = 256
RWKV_HEADS = 4
RWKV_LORA = 64
RWKV_GATE_LORA = 128
RWKV_COLS = 1152
MLA_HEADS = 8
MLA_NOPE = 64
MLA_ROPE = 32
MLA_V = 64
MLA_Q_RANK = 384
MLA_KV_RANK = 256
MLA_WIDTH = 512
MLA_HEAD_PAD = 128
MLA_TQ = 2048
MLA_QCOLS = 512
MLA_TK = 512
MLA_ONES_ROWS = 16
MLA_BOUNDED_UNROLL = 4
MLA_BOUND_MARGIN = 1.01
MLA_BOUND_LIMIT = 50.0
RET_WIDTH = 256
RET_HEADS = 4
RET_CHUNK = 128
RET_BATCH_PER_STEP = 4
RWKV_CHUNK = 64
RWKV_BATCH_PER_STEP = 4
RWKV_PASSES = 1
D_FF = 2816
FFN_TM = 1024
FFN_TF = 256
N_MOD = 9
ROPE_BASE = 10000.0
NORM_EPS = 1e-6
RWKV_LN_EPS = 64e-5
RET_LN_EPS = 1e-5

VMEM_LIMIT = 48 * 1024 * 1024


def _cparams(*sem):
    return pltpu.CompilerParams(dimension_semantics=sem, vmem_limit_bytes=VMEM_LIMIT)


def _sigmoid(x):
    return 1.0 / (1.0 + jnp.exp(-x))


def _dot(a, b):
    return jnp.dot(a, b, preferred_element_type=F32)


def _split(a):
    hi = a.astype(BF16)
    lo = (a - hi.astype(F32)).astype(BF16)
    return hi, lo


def _head_sums(a, sel):
    ah, al = _split(a)
    sel = sel.astype(BF16)
    return _dot(ah, sel) + _dot(al, sel)


def _dg3(a, b, dims):
    ah, al = _split(a)
    bh, bl = _split(b)
    dn = (dims, ((), ()))
    out = lax.dot_general(ah, bh, dn, preferred_element_type=F32)
    out = out + lax.dot_general(ah, bl, dn, preferred_element_type=F32)
    out = out + lax.dot_general(al, bh, dn, preferred_element_type=F32)
    return out


_NN = ((1,), (0,))
_NT = ((1,), (1,))
_TN = ((0,), (0,))


def _rms_mod(x, sh, sc):
    ms = jnp.mean(x * x, axis=-1, keepdims=True)
    return (x * lax.rsqrt(ms + NORM_EPS)) * (1.0 + sc) + sh


def _mod_kernel(c_ref, w_ref, b_ref, o_ref):
    c = c_ref[...]
    cond = c * _sigmoid(c)
    o_ref[0] = _dg3(cond, w_ref[0], _NN) + b_ref[0]


def _mod_call(c, w_ada, b_ada):
    L, D, N = w_ada.shape
    B = c.shape[0]
    tn = 1152
    return pl.pallas_call(
        _mod_kernel,
        grid=(L, N // tn),
        in_specs=[
            pl.BlockSpec((B, D), lambda l, j: (0, 0)),
            pl.BlockSpec((1, D, tn), lambda l, j: (l, 0, j)),
            pl.BlockSpec((1, 1, tn), lambda l, j: (l, 0, j)),
        ],
        out_specs=pl.BlockSpec((1, B, tn), lambda l, j: (l, 0, j)),
        out_shape=jax.ShapeDtypeStruct((L, B, N), F32),
        compiler_params=_cparams("parallel", "parallel"),
        name="adaln_mod",
    )(c, w_ada, b_ada.reshape(L, 1, N))


def _rope_kernel(pos_ref, inv_ref, mc_ref, m1_ref, ma_ref, mb_ref, o_ref, ot_ref):
    pos = pos_ref[0].astype(F32)
    for i in range(2):
        ang = pos * inv_ref[i:i + 1, :]
        cos = jnp.cos(ang)
        sin = jnp.sin(ang)
        o_ref[i, 0, :, 0:128] = cos * mc_ref[i:i + 1, :] + m1_ref[i:i + 1, :]
        o_ref[i, 0, :, 128:256] = sin * ma_ref[i:i + 1, :]
        o_ref[i, 0, :, 256:384] = sin * mb_ref[i:i + 1, :]
    ot_ref[0] = o_ref[0, 0].T


def _rope_consts():
    lane = jnp.arange(128)
    inv_m = ROPE_BASE ** (-jnp.arange(0, MLA_ROPE, 2, dtype=F32) / MLA_ROPE)
    in_rope = (lane >= 64) & (lane < 96)
    inv0 = jnp.where(in_rope, inv_m[(lane - 64) % 16], 0.0)
    mc0 = in_rope.astype(F32)
    m10 = (lane < 64).astype(F32)
    ma0 = jnp.where((lane >= 64) & (lane < 80), -1.0, 0.0)
    mb0 = jnp.where((lane >= 80) & (lane < 96), 1.0, 0.0)
    inv_r = ROPE_BASE ** (-jnp.arange(0, HEAD_DIM, 2, dtype=F32) / HEAD_DIM)
    inv1 = inv_r[lane % 32]
    mc1 = jnp.ones((128,), F32)
    m11 = jnp.zeros((128,), F32)
    ma1 = jnp.where((lane % 64) < 32, -1.0, 0.0)
    mb1 = jnp.where((lane % 64) >= 32, 1.0, 0.0)
    st = lambda a, b: jnp.stack([a, b]).astype(F32)
    return st(inv0, inv1), st(mc0, mc1), st(m10, m11), st(ma0, ma1), st(mb0, mb1)


def _rope_call(positions):
    B, S = positions.shape
    ts = min(S, 512)
    consts = _rope_consts()
    cspec = pl.BlockSpec((2, 128), lambda b, i: (0, 0))
    return pl.pallas_call(
        _rope_kernel,
        grid=(B, S // ts),
        in_specs=[pl.BlockSpec((1, ts, 1), lambda b, i: (b, i, 0))] + [cspec] * 5,
        out_specs=[pl.BlockSpec((2, 1, ts, 384), lambda b, i: (0, b, i, 0)),
                   pl.BlockSpec((1, 384, ts), lambda b, i: (b, 0, i))],
        out_shape=[jax.ShapeDtypeStruct((2, B, S, 384), F32), jax.ShapeDtypeStruct((B, 384, S), F32)],
        compiler_params=_cparams("parallel", "parallel"),
        name="rope_tables",
    )(positions.reshape(B, S, 1), *consts)


def _apply_rope(x, tab, shift):
    n = x.shape[1] // 128
    width = x.shape[1]
    cc = jnp.tile(tab[:, 0:128], (1, n))
    sa = jnp.tile(tab[:, 128:256], (1, n))
    sb = jnp.tile(tab[:, 256:384], (1, n))
    return x * cc + pltpu.roll(x, width - shift, axis=1) * sa + pltpu.roll(x, shift, axis=1) * sb


def _ffn_kernel(x_ref, sh_ref, sc_ref, g_ref, wi_ref, wo_ref, fg_ref, o_ref, *, final):
    x = x_ref[0]
    h = _rms_mod(x, sh_ref[0], sc_ref[0]).astype(BF16)
    acc = None
    for c in range(D_FF // FFN_TF):
        cols = slice(c * FFN_TF, (c + 1) * FFN_TF)
        gate = _dot(h, wi_ref[:, cols])
        up = _dot(h, wi_ref[:, D_FF + c * FFN_TF:D_FF + (c + 1) * FFN_TF])
        act = (gate * _sigmoid(gate) * up).astype(BF16)
        part = _dot(act, wo_ref[cols, :])
        acc = part if acc is None else acc + part
    y = x + (0.5 * g_ref[0]) * acc
    if final:
        ms = jnp.mean(y * y, axis=-1, keepdims=True)
        y = (y * lax.rsqrt(ms + NORM_EPS)) * fg_ref[...]
    o_ref[0] = y


def _ffn_call(x, sh, sc, g, w_in_bf, w_out_bf, layer, final_g, final):
    B, S, D = x.shape
    tm = min(S, FFN_TM)
    vec = pl.BlockSpec((1, 1, D), lambda b, i: (b, 0, 0))
    resident = lambda shape: pl.BlockSpec((None,) + shape, lambda b, i: (layer, 0, 0), pipeline_mode=pl.Buffered(1))
    return pl.pallas_call(
        functools.partial(_ffn_kernel, final=final),
        grid=(B, S // tm),
        in_specs=[
            pl.BlockSpec((1, tm, D), lambda b, i: (b, i, 0)),
            vec, vec, vec,
            resident((D, 2 * D_FF)), resident((D_FF, D)),
            pl.BlockSpec((1, D), lambda b, i: (0, 0)),
        ],
        out_specs=pl.BlockSpec((1, tm, D), lambda b, i: (b, i, 0)),
        out_shape=jax.ShapeDtypeStruct((B, S, D), F32),
        compiler_params=_cparams("parallel", "parallel"),
        name="ffn",
    )(x, sh, sc, g, w_in_bf, w_out_bf, final_g)


ZB_COLS = MLA_Q_RANK + MLA_KV_RANK + MLA_HEAD_PAD
ZC_COLS = 4 * RET_WIDTH
Z_COLS = RWKV_COLS + ZB_COLS + ZC_COLS


def _inproj_kernel(x_ref, sh_ref, sc_ref, w_ref, tab_ref, za_ref, zb_ref, zc_ref):
    h = _rms_mod(x_ref[0], sh_ref[0], sc_ref[0]).astype(BF16)
    z = _dot(h, w_ref[...])
    za_ref[0] = z[:, 0:RWKV_COLS]
    zb_ref[0] = z[:, RWKV_COLS:RWKV_COLS + ZB_COLS].astype(BF16)
    o = RWKV_COLS + ZB_COLS
    tab = tab_ref[0, 0]
    zc_ref[0, :, 0:256] = _apply_rope(z[:, o:o + 256], tab, 32).astype(BF16)
    zc_ref[0, :, 256:512] = (_apply_rope(z[:, o + 256:o + 512], tab, 32) * (HEAD_DIM ** -0.5)).astype(BF16)
    zc_ref[0, :, 512:1024] = z[:, o + 512:Z_COLS].astype(BF16)


def _inproj_call(x, sh, sc, wz_bf, tabs):
    B, S, D = x.shape
    tm = min(S, 512)
    vec = pl.BlockSpec((1, 1, D), lambda b, i: (b, 0, 0))
    out = lambda n: pl.BlockSpec((1, tm, n), lambda b, i: (b, i, 0))
    return pl.pallas_call(
        _inproj_kernel,
        grid=(B, S // tm),
        in_specs=[
            pl.BlockSpec((1, tm, D), lambda b, i: (b, i, 0)),
            vec, vec,
            pl.BlockSpec((D, Z_COLS), lambda b, i: (0, 0)),
            pl.BlockSpec((1, 1, tm, 384), lambda b, i: (1, b, i, 0)),
        ],
        out_specs=[out(RWKV_COLS), out(ZB_COLS), out(ZC_COLS)],
        out_shape=[jax.ShapeDtypeStruct((B, S, n), dt)
                   for n, dt in ((RWKV_COLS, F32), (ZB_COLS, BF16), (ZC_COLS, BF16))],
        compiler_params=_cparams("parallel", "parallel"),
        name="inproj",
    )(x, sh, sc, wz_bf, tabs)


def _rwkv_prep_kernel(z_ref, zp_ref, zn_ref, mu_ref, w0_ref, wup_ref, a0_ref, aup_ref, gup_ref,
                      kk_ref, ka_ref, rk_ref, ones_ref, pc_ref, pd_ref, pw_ref, pg_ref, *, ts, nt):
    i = pl.program_id(1)
    z = z_ref[0]
    row = lax.broadcasted_iota(jnp.int32, (ts, 1), 0)
    prev_edge = jnp.where(i > 0, zp_ref[0, 7:8, :], 0.0)
    next_edge = jnp.where(i < nt - 1, zn_ref[0, 0:1, :], 0.0)
    prev = jnp.where(row == 0, prev_edge, pltpu.roll(z, 1, axis=0))
    nxt = jnp.where(row == ts - 1, next_edge, pltpu.roll(z, ts - 1, axis=0))
    zs = z + mu_ref[...] * (0.5 * (prev + nxt) - z)

    r = zs[:, 0:256]
    k = zs[:, 256:512]
    v = zs[:, 512:768]
    g_lo = zs[:, 768:896]
    w_lo = zs[:, 896:1024]
    a_lo = zs[:, 1024:1152]

    w_raw = w0_ref[...] + _dg3(jnp.tanh(w_lo), wup_ref[...], _NN)
    logw = (-math.exp(-0.5)) * _sigmoid(w_raw)
    a = _sigmoid(a0_ref[...] + _dg3(a_lo, aup_ref[...], _NN))
    g = _dg3(_sigmoid(g_lo), gup_ref[...], _NN)

    ones_bd = ones_ref[...]
    kk0 = k * kk_ref[...]
    nrm = jnp.sqrt(_head_sums(kk0 * kk0, ones_bd))
    kk = kk0 / jnp.maximum(nrm, 1e-12)
    bonus = _head_sums(r * k * rk_ref[...], ones_bd) * v

    pc_ref[0, :, 0:256] = r.astype(BF16)
    pc_ref[0, :, 256:512] = v.astype(BF16)
    pc_ref[0, :, 512:768] = kk.astype(BF16)
    for d in range(2):
        a_d = a[:, 256 * d:256 * d + 256]
        pd_ref[d, 0, :, 0:256] = (k * (1.0 + (a_d - 1.0) * ka_ref[...])).astype(BF16)
        pd_ref[d, 0, :, 256:512] = (kk * a_d).astype(BF16)
        pw_ref[d, 0] = logw[:, 256 * d:256 * d + 256]
    pg_ref[0, :, 0:256] = g.astype(BF16)
    pg_ref[0, :, 256:512] = bonus.astype(BF16)


def _rwkv_prep_call(za, mu, w0, wup_bd, a0, aup_bd, gup, k_k, k_a, r_k, ones_bd):
    B, S, _ = za.shape
    ts = min(S, 512)
    nt = S // ts
    nb8 = S // 8
    const = lambda shape: pl.BlockSpec(shape, lambda b, i: (0,) * len(shape))
    return pl.pallas_call(
        functools.partial(_rwkv_prep_kernel, ts=ts, nt=nt),
        grid=(B, nt),
        in_specs=[
            pl.BlockSpec((1, ts, RWKV_COLS), lambda b, i: (b, i, 0)),
            pl.BlockSpec((1, 8, RWKV_COLS), lambda b, i: (b, jnp.maximum(i * (ts // 8) - 1, 0), 0)),
            pl.BlockSpec((1, 8, RWKV_COLS), lambda b, i: (b, jnp.minimum((i + 1) * (ts // 8), nb8 - 1), 0)),
            const((1, RWKV_COLS)),
            const((1, 512)), const((128, 512)), const((1, 512)), const((128, 512)), const((128, 256)),
            const((1, 256)), const((1, 256)), const((1, 256)), const((256, 256)),
        ],
        out_specs=[
            pl.BlockSpec((1, ts, 768), lambda b, i: (b, i, 0)),
            pl.BlockSpec((2, 1, ts, 512), lambda b, i: (0, b, i, 0)),
            pl.BlockSpec((2, 1, ts, 256), lambda b, i: (0, b, i, 0)),
            pl.BlockSpec((1, ts, 512), lambda b, i: (b, i, 0)),
        ],
        out_shape=[
            jax.ShapeDtypeStruct((B, S, 768), BF16),
            jax.ShapeDtypeStruct((2, B, S, 512), BF16),
            jax.ShapeDtypeStruct((2, B, S, 256), F32),
            jax.ShapeDtypeStruct((B, S, 512), BF16),
        ],
        compiler_params=_cparams("parallel", "parallel"),
        name="rwkv_prep",
    )(za, za, za, mu, w0, wup_bd, a0, aup_bd, gup, k_k, k_a, r_k, ones_bd)


def _rwkv_dir_operands(direction, pc, pd, logw):
    C = RWKV_CHUNK
    r = pc[:, 0:256]
    v = pc[:, 256:512]
    kk = pc[:, 512:768]
    kmod = pd[:, 0:256]
    kka = pd[:, 256:512]

    row = lax.broadcasted_iota(jnp.int32, (C, C), 0)
    col = lax.broadcasted_iota(jnp.int32, (C, C), 1)
    tri = jnp.where((col <= row) if direction == 0 else (col >= row), 1.0, 0.0).astype(BF16)
    lw_hi, lw_lo = _split(logw)
    l_in = _dot(tri, lw_hi) + _dot(tri, lw_lo)
    l_ex = l_in - logw
    l_tot = l_in[C - 1:C, :] if direction == 0 else l_in[0:1, :]
    e_ex = jnp.exp(l_ex)
    e_neg = jnp.exp(-l_in)
    e_tot = jnp.exp(l_tot - l_in)
    kkg = kk * e_ex
    bbar = kka * e_neg
    kbar = kmod * e_neg
    bg = kka * e_tot
    kg = kmod * e_tot
    rg = r * jnp.exp(l_in) if direction == 0 else r * e_ex
    g_tot = jnp.exp(l_tot)
    return dict(kkg=kkg, bbar=bbar, kbar=kbar, bg=bg, kg=kg, rg=rg, v=v, g_tot=g_tot)


def _operand(a, passes):
    hi = a.astype(BF16)
    lo = (a - hi.astype(F32)).astype(BF16) if passes > 1 else None
    return hi, lo


def _mm(a, b, dims):
    dn = (dims, ((), ()))
    out = lax.dot_general(a[0], b[0], dn, preferred_element_type=F32)
    if b[1] is not None:
        out = out + lax.dot_general(a[0], b[1], dn, preferred_element_type=F32)
    if a[1] is not None:
        out = out + lax.dot_general(a[1], b[0], dn, preferred_element_type=F32)
    return out


def _rwkv_scan_kernel(pcf_ref, pcb_ref, pdf_ref, pdb_ref, pwf_ref, pwb_ref, yf_ref, yb_ref, s_ref):
    C = RWKV_CHUNK
    P = RWKV_PASSES

    @pl.when(pl.program_id(1) == 0)
    def _():
        s_ref[...] = jnp.zeros_like(s_ref)

    row = lax.broadcasted_iota(jnp.int32, (C, C), 0)
    col = lax.broadcasted_iota(jnp.int32, (C, C), 1)
    eye = col == row
    ident = jnp.where(eye, 1.0, 0.0)
    nb = pcf_ref.shape[0]
    ops = {}
    for bb in range(nb):
        ops[bb, 0] = _rwkv_dir_operands(0, pcf_ref[bb].astype(F32), pdf_ref[0, bb].astype(F32), pwf_ref[0, bb])
        ops[bb, 1] = _rwkv_dir_operands(1, pcb_ref[bb].astype(F32), pdb_ref[0, bb].astype(F32), pwb_ref[0, bb])
    earlier = (col < row, col > row)
    ymask = (col <= row, col > row)
    units = [(bb, d, h) for bb in range(nb) for d in range(2) for h in range(RWKV_HEADS)]
    nu = len(units)
    sl = lambda h: slice(64 * h, 64 * h + 64)
    s0 = [s_ref[u] for u in units]

    def head(name):
        return [_operand(ops[bb, d][name][:, sl(h)], P) for bb, d, h in units]

    kkg, bbar, kbar, bg, kg, rg, v = (head(n) for n in ("kkg", "bbar", "kbar", "bg", "kg", "rg", "v"))
    akk = [jnp.where(earlier[d], _mm(kkg[i], bbar[i], _NT), 0.0) for i, (_, d, _) in enumerate(units)]
    bk = [jnp.where(earlier[d], _mm(kkg[i], kbar[i], _NT), 0.0) for i, (_, d, _) in enumerate(units)]
    ark = [jnp.where(ymask[d], _mm(rg[i], bbar[i], _NT), 0.0) for i, (_, d, _) in enumerate(units)]
    brk = [jnp.where(ymask[d], _mm(rg[i], kbar[i], _NT), 0.0) for i, (_, d, _) in enumerate(units)]
    pw = [-a for a in akk]
    t_inv = [ident + x for x in pw]
    for _ in range(int(math.log2(C)) - 1):
        pw_o = [_operand(x, P) for x in pw]
        pw = [_mm(x, x, _NN) for x in pw_o]
        pw_o = [_operand(x, P) for x in pw]
        t_inv = [t + _mm(_operand(t, P), x, _NN) for t, x in zip(t_inv, pw_o)]
    t_o = [_operand(t, P) for t in t_inv]
    bkv = [_mm(_operand(b, P), x, _NN) for b, x in zip(bk, v)]
    w_m = [_mm(t, x, _NN) for t, x in zip(t_o, kkg)]
    u_m = [_mm(t, _operand(x, P), _NN) for t, x in zip(t_o, bkv)]
    w_o = [_operand(x, P) for x in w_m]
    u_o = [_operand(x, P) for x in u_m]
    ark_o = [_operand(x, P) for x in ark]
    brk_o = [_operand(x, P) for x in brk]
    m_m = [jnp.where(eye, ops[bb, d]["g_tot"][:, sl(h)], 0.0) - _mm(w_o[i], bg[i], _TN)
           for i, (bb, d, h) in enumerate(units)]
    n_m = [_mm(v[i], kg[i], _TN) - _mm(u_o[i], bg[i], _TN) for i in range(nu)]
    q_m = [ops[bb, d]["rg"][:, sl(h)] - _mm(ark_o[i], w_o[i], _NN) for i, (bb, d, h) in enumerate(units)]
    y_i = [_mm(brk_o[i], v[i], _NN) - _mm(ark_o[i], u_o[i], _NN) for i in range(nu)]
    s_o = [_operand(s, P) for s in s0]
    y = [y_i[i] + _mm(_operand(q_m[i], P), s_o[i], _NT) for i in range(nu)]
    s_new = [_mm(s_o[i], _operand(m_m[i], P), _NN) + n_m[i] for i in range(nu)]
    for i, (bb, d, h) in enumerate(units):
        (yf_ref if d == 0 else yb_ref)[bb, :, sl(h)] = y[i]
        s_ref[bb, d, h] = s_new[i]


def _rwkv_scan_call(pc, pd, pw):
    B, S, _ = pc.shape
    C = RWKV_CHUNK
    nc = S // C
    nb = math.gcd(B, RWKV_BATCH_PER_STEP)
    return pl.pallas_call(
        _rwkv_scan_kernel,
        grid=(B // nb, nc),
        in_specs=[
            pl.BlockSpec((nb, C, 768), lambda b, c: (b, c, 0)),
            pl.BlockSpec((nb, C, 768), lambda b, c: (b, nc - 1 - c, 0)),
            pl.BlockSpec((1, nb, C, 512), lambda b, c: (0, b, c, 0)),
            pl.BlockSpec((1, nb, C, 512), lambda b, c: (1, b, nc - 1 - c, 0)),
            pl.BlockSpec((1, nb, C, 256), lambda b, c: (0, b, c, 0)),
            pl.BlockSpec((1, nb, C, 256), lambda b, c: (1, b, nc - 1 - c, 0)),
        ],
        out_specs=[
            pl.BlockSpec((nb, C, 256), lambda b, c: (b, c, 0)),
            pl.BlockSpec((nb, C, 256), lambda b, c: (b, nc - 1 - c, 0)),
        ],
        out_shape=[jax.ShapeDtypeStruct((B, S, 256), F32)] * 2,
        scratch_shapes=[pltpu.VMEM((nb, 2, RWKV_HEADS, HEAD_DIM, HEAD_DIM), F32)],
        compiler_params=_cparams("parallel", "arbitrary"),
        name="rwkv_scan",
    )(pc, pc, pd, pd, pw, pw)


def _mla_prep_kernel(zb_ref, tab_ref, tabt_ref, qg_ref, wqt_ref, kg_ref, wk_ref, wvt_ref, hsel_ref,
                     qt_ref, k_ref, vt_ref, kn_ref):
    zb = zb_ref[0].astype(F32)
    tab = tab_ref[0, 0]
    cq = zb[:, 0:MLA_Q_RANK]
    ckv = zb[:, MLA_Q_RANK:MLA_Q_RANK + MLA_KV_RANK]
    kpe = zb[:, MLA_Q_RANK + MLA_KV_RANK:ZB_COLS]

    def rms(t):
        return t * lax.rsqrt(jnp.mean(t * t, axis=-1, keepdims=True) + NORM_EPS)

    cqn = (rms(cq) * qg_ref[...]).astype(BF16)
    ckvn = (rms(ckv) * kg_ref[...]).astype(BF16)
    nt_dims = (_NT, ((), ()))
    qt = lax.dot_general(wqt_ref[...], cqn, nt_dims, preferred_element_type=F32)
    rows = qt.shape[0]
    tabt = tabt_ref[0]
    cc, sa, sb = (jnp.tile(tabt[128 * j:128 * j + 128], (MLA_HEADS, 1)) for j in range(3))
    qt = qt * cc + pltpu.roll(qt, rows - 16, axis=0) * sa + pltpu.roll(qt, 16, axis=0) * sb
    k = _dot(ckvn, wk_ref[...])
    kpe = _apply_rope(kpe, tab, 16)
    k = k + jnp.tile(kpe, (1, MLA_HEADS))
    qt_ref[0] = (qt * (math.log2(math.e) * (MLA_NOPE + MLA_ROPE) ** -0.5)).astype(BF16)
    kb = k.astype(BF16)
    k_ref[0] = kb
    kf = kb.astype(F32)
    k_sq = jnp.max(_head_sums(kf * kf, hsel_ref[...]), axis=0, keepdims=True)
    kn_ref[0, 0] = jnp.broadcast_to(k_sq, (8, 128))
    vt_ref[0, 0] = lax.dot_general(wvt_ref[...], ckvn, nt_dims, preferred_element_type=F32).astype(BF16)


def _mla_key_tile(S):
    return min(S, MLA_TK)


def _mla_prep_call(zb, tabs, tab_t, qg, wqt_bf, kg, wk_bf, wvt_bf, head_sel):
    B, S, _ = zb.shape
    ts = _mla_key_tile(S)
    HP = MLA_HEADS * MLA_HEAD_PAD
    const = lambda shape: pl.BlockSpec(shape, lambda b, i: (0,) * len(shape))
    return pl.pallas_call(
        _mla_prep_kernel,
        grid=(B, S // ts),
        in_specs=[
            pl.BlockSpec((1, ts, ZB_COLS), lambda b, i: (b, i, 0)),
            pl.BlockSpec((1, 1, ts, 384), lambda b, i: (0, b, i, 0)),
            pl.BlockSpec((1, 384, ts), lambda b, i: (b, 0, i)),
            const((1, MLA_Q_RANK)), const((HP, MLA_Q_RANK)),
            const((1, MLA_KV_RANK)), const((MLA_KV_RANK, HP)), const((MLA_WIDTH, MLA_KV_RANK)),
            const((HP, 128)),
        ],
        out_specs=[
            pl.BlockSpec((1, HP, ts), lambda b, i: (b, 0, i)),
            pl.BlockSpec((1, ts, HP), lambda b, i: (b, i, 0)),
            pl.BlockSpec((1, 1, MLA_WIDTH, ts), lambda b, i: (b, i, 0, 0)),
            pl.BlockSpec((1, 1, 8, 128), lambda b, i: (b, i, 0, 0)),
        ],
        out_shape=[
            jax.ShapeDtypeStruct((B, HP, S), BF16),
            jax.ShapeDtypeStruct((B, S, HP), BF16),
            jax.ShapeDtypeStruct((B, S // ts, MLA_WIDTH, ts), BF16),
            jax.ShapeDtypeStruct((B, S // ts, 8, 128), F32),
        ],
        compiler_params=_cparams("parallel", "parallel"),
        name="mla_prep",
    )(zb, tabs, tab_t, qg, wqt_bf, kg, wk_bf, wvt_bf, head_sel)


def _mla_attn_kernel(qt_ref, k_ref, vt_ref, kn_ref, o_ref, *, tk, nk):
    tq = qt_ref.shape[2]
    cw = min(tq, MLA_QCOLS)
    ncol = tq // cw
    chunks = [(hh, c) for hh in range(2) for c in range(ncol)]
    ones = jnp.ones((MLA_ONES_ROWS, tk), BF16)
    acc0 = jnp.zeros((MLA_V + MLA_ONES_ROWS, cw), F32)

    def scores(i):
        ks = pl.multiple_of(i * tk, tk)
        return [_dot(k_ref[0, pl.ds(ks, tk), 128 * hh:128 * hh + 128],
                     qt_ref[0, 128 * hh:128 * hh + 128, c * cw:(c + 1) * cw]) for hh, c in chunks]

    def values(i, hh):
        return jnp.concatenate([vt_ref[0, i, 64 * hh:64 * hh + 64, :], ones], axis=0)

    k_sq = jnp.max(kn_ref[0], axis=(0, 1), keepdims=True)[0]
    lane = lax.broadcasted_iota(jnp.int32, k_sq.shape, 1)
    head0 = 2 * pl.program_id(1)
    bounds = []
    for hh, c in chunks:
        k_h = jnp.sum(jnp.where(lane == head0 + hh, k_sq, 0.0), axis=1, keepdims=True)
        qf = qt_ref[0, 128 * hh:128 * hh + 128, c * cw:(c + 1) * cw].astype(F32)
        bounds.append(jnp.sqrt(jnp.sum(qf * qf, axis=0, keepdims=True) * k_h) * MLA_BOUND_MARGIN)
    worst = functools.reduce(jnp.maximum, [jnp.max(b) for b in bounds])

    def bounded_offset():
        def body(i, carry):
            ps = [jnp.exp2(s - bounds[j]).astype(BF16) for j, s in enumerate(scores(i))]
            return tuple(carry[j] + _dot(values(i, hh), ps[j]) for j, (hh, c) in enumerate(chunks))

        return lax.fori_loop(0, nk, body, (acc0,) * len(chunks), unroll=math.gcd(nk, MLA_BOUNDED_UNROLL))

    def running_max():
        def body(i, carry):
            stats = []
            for j, s in enumerate(scores(i)):
                m = carry[j][0]
                m_new = jnp.maximum(m, jnp.max(s, axis=0, keepdims=True))
                stats.append((m_new, jnp.exp2(m - m_new), jnp.exp2(s - m_new).astype(BF16)))
            new = []
            for j, (hh, c) in enumerate(chunks):
                m_new, alpha, p = stats[j]
                new.append((m_new, alpha * carry[j][1] + _dot(values(i, hh), p)))
            return tuple(new)

        init = (jnp.full((1, cw), -jnp.inf, F32), acc0)
        return tuple(r[1] for r in lax.fori_loop(0, nk, body, (init,) * len(chunks)))

    res = lax.cond(worst <= MLA_BOUND_LIMIT, bounded_offset, running_max)
    out_t = [jnp.concatenate([res[hh * ncol + c][0:MLA_V] / res[hh * ncol + c][MLA_V:MLA_V + 1]
                              for c in range(ncol)], axis=1) for hh in range(2)]
    o_ref[0] = jnp.concatenate(out_t, axis=0).T.astype(BF16)


def _mla_attn_call(qt, k, vt, kn):
    B, S, _ = k.shape
    tq = min(S, MLA_TQ)
    tk = _mla_key_tile(S)
    nk = S // tk
    return pl.pallas_call(
        functools.partial(_mla_attn_kernel, tk=tk, nk=nk),
        grid=(B, MLA_HEADS // 2, S // tq),
        in_specs=[
            pl.BlockSpec((1, 256, tq), lambda b, h, i: (b, h, i)),
            pl.BlockSpec((1, S, 256), lambda b, h, i: (b, 0, h)),
            pl.BlockSpec((1, nk, 128, tk), lambda b, h, i: (b, 0, h, 0)),
            pl.BlockSpec((1, nk, 8, 128), lambda b, h, i: (b, 0, 0, 0)),
        ],
        out_specs=pl.BlockSpec((1, tq, 128), lambda b, h, i: (b, i, h)),
        out_shape=jax.ShapeDtypeStruct((B, S, MLA_WIDTH), BF16),
        compiler_params=_cparams("parallel", "parallel", "arbitrary"),
        name="mla_attn",
    )(qt, k, vt, kn)


def _ret_dir_operands(direction, zc, lrv, lrh):
    C = RET_CHUNK
    q = zc[:, 0:256]
    k = zc[:, 256:512]
    v = zc[:, 512:768]
    lgv = -jnp.exp(lrv)
    lgh = -jnp.exp(lrh)
    pos = lax.broadcasted_iota(jnp.int32, (C, 1), 0).astype(F32)
    row = lax.broadcasted_iota(jnp.int32, (C, C), 0)
    col = lax.broadcasted_iota(jnp.int32, (C, C), 1)
    if direction == 0:
        kw = k * jnp.exp(lgv * (C - 1.0 - pos))
        qw = q * jnp.exp(lgv * (pos + 1.0))
        mask = col <= row
        dist = (row - col).astype(F32)
    else:
        kw = k * jnp.exp(lgv * pos)
        qw = q * jnp.exp(lgv * (C - pos))
        mask = col > row
        dist = (col - row).astype(F32)
    dist = jnp.maximum(dist, 0.0)
    dmats = [jnp.where(mask, jnp.exp(lgh[h:h + 1, :] * dist), 0.0) for h in range(RET_HEADS)]
    decays = [jnp.exp(lgh[h:h + 1, 0:64] * C) for h in range(RET_HEADS)]
    qb, kb, vb, kwb, qwb = (t.astype(BF16) for t in (q, k, v, kw, qw))
    return dict(q=qb, k=kb, v=vb, kw=kwb, qw=qwb, dmat=dmats, decay=decays)


def _ret_kernel(zf_ref, zb_ref, lrv_ref, lrh_ref, yf_ref, yb_ref, r_ref):
    @pl.when(pl.program_id(1) == 0)
    def _():
        r_ref[...] = jnp.zeros_like(r_ref)

    nb = zf_ref.shape[0]
    ops = {}
    for bb in range(nb):
        ops[bb, 0] = _ret_dir_operands(0, zf_ref[bb].astype(F32), lrv_ref[0:1, :], lrh_ref[0:4, :])
        ops[bb, 1] = _ret_dir_operands(1, zb_ref[bb].astype(F32), lrv_ref[1:2, :], lrh_ref[4:8, :])
    units = [(bb, d, h) for bb in range(nb) for d in range(2) for h in range(RET_HEADS)]
    sl = lambda h: slice(64 * h, 64 * h + 64)
    r0 = [r_ref[u] for u in units]
    sc = [lax.dot_general(ops[bb, d]["q"][:, sl(h)], ops[bb, d]["k"][:, sl(h)], (_NT, ((), ())),
                          preferred_element_type=F32) * ops[bb, d]["dmat"][h] for bb, d, h in units]
    kv = [lax.dot_general(ops[bb, d]["kw"][:, sl(h)], ops[bb, d]["v"][:, sl(h)], (_TN, ((), ())),
                          preferred_element_type=F32) for bb, d, h in units]
    cross = [_dot(ops[bb, d]["qw"][:, sl(h)], r0[i].astype(BF16)) for i, (bb, d, h) in enumerate(units)]
    inner = [_dot(sc[i].astype(BF16), ops[bb, d]["v"][:, sl(h)]) for i, (bb, d, h) in enumerate(units)]
    for i, (bb, d, h) in enumerate(units):
        (yf_ref if d == 0 else yb_ref)[bb, :, sl(h)] = inner[i] + cross[i]
        r_ref[bb, d, h] = r0[i] * ops[bb, d]["decay"][h] + kv[i]


def _ret_call(zc, lr_vec, lr_heads):
    B, S, _ = zc.shape
    C = RET_CHUNK
    nc = S // C
    nb = math.gcd(B, RET_BATCH_PER_STEP)
    const = lambda shape: pl.BlockSpec(shape, lambda b, c: (0,) * len(shape))
    return pl.pallas_call(
        _ret_kernel,
        grid=(B // nb, nc),
        in_specs=[
            pl.BlockSpec((nb, C, ZC_COLS), lambda b, c: (b, c, 0)),
            pl.BlockSpec((nb, C, ZC_COLS), lambda b, c: (b, nc - 1 - c, 0)),
            const((2, 256)), const((8, 128)),
        ],
        out_specs=[
            pl.BlockSpec((nb, C, 256), lambda b, c: (b, c, 0)),
            pl.BlockSpec((nb, C, 256), lambda b, c: (b, nc - 1 - c, 0)),
        ],
        out_shape=[jax.ShapeDtypeStruct((B, S, 256), F32)] * 2,
        scratch_shapes=[pltpu.VMEM((nb, 2, RET_HEADS, HEAD_DIM, HEAD_DIM), F32)],
        compiler_params=_cparams("parallel", "arbitrary"),
        name="retention",
    )(zc, zc, lr_vec, lr_heads)


def _head_norm(y, avg_bd, eps):
    mu = _head_sums(y, avg_bd)
    d = y - mu
    var = _head_sums(d * d, avg_bd)
    return d * lax.rsqrt(var + eps)


def _outproj_kernel(x_ref, g2_ref, yf_ref, yb_ref, pg_ref, lng_ref, lnb_ref, om_ref, rf_ref, rb_ref,
                    gate_ref, gng_ref, avg_ref, w_ref, o_ref):
    avg_bd = avg_ref[...]
    y = _head_norm(yf_ref[0] + yb_ref[0], avg_bd, RWKV_LN_EPS) * lng_ref[...] + lnb_ref[...]
    pg = pg_ref[0].astype(F32)
    o_a = (y + pg[:, 256:512]) * pg[:, 0:256]
    yr = _head_norm(rf_ref[0] + rb_ref[0], avg_bd, RET_LN_EPS) * gng_ref[...]
    gate = gate_ref[0].astype(F32)
    o_c = (gate * _sigmoid(gate)) * yr
    mixed = _dot(o_a.astype(BF16), w_ref[0:RWKV_WIDTH, :])
    mixed = mixed + _dot(om_ref[0], w_ref[RWKV_WIDTH:RWKV_WIDTH + MLA_WIDTH, :])
    mixed = mixed + _dot(o_c.astype(BF16), w_ref[RWKV_WIDTH + MLA_WIDTH:D_MODEL, :])
    o_ref[0] = x_ref[0] + g2_ref[0] * mixed


def _outproj_call(x, g2, yf, yb, pg, ln_g, ln_b, o_mla, rf, rb, zc, gn_g, avg_bd, w_out_bf, layer):
    B, S, D = x.shape
    tm = min(S, 512)
    tok = lambda n: pl.BlockSpec((1, tm, n), lambda b, i: (b, i, 0))
    const = lambda shape: pl.BlockSpec(shape, lambda b, i: (0,) * len(shape))
    return pl.pallas_call(
        _outproj_kernel,
        grid=(B, S // tm),
        in_specs=[
            tok(D),
            pl.BlockSpec((1, 1, D), lambda b, i: (b, 0, 0)),
            tok(256), tok(256), tok(512), const((1, 256)), const((1, 256)),
            tok(512), tok(256), tok(256),
            pl.BlockSpec((1, tm, 256), lambda b, i: (b, i, 3)),
            const((1, 256)), const((256, 256)),
            pl.BlockSpec((None, D, D), lambda b, i: (layer, 0, 0)),
        ],
        out_specs=tok(D),
        out_shape=jax.ShapeDtypeStruct((B, S, D), F32),
        compiler_params=_cparams("parallel", "parallel"),
        name="outproj",
    )(x, g2, yf, yb, pg, ln_g, ln_b, o_mla, rf, rb, zc, gn_g, avg_bd, w_out_bf)


def _block_diag2(w):
    z = jnp.zeros_like(w[0])
    return jnp.concatenate([jnp.concatenate([w[0], z], axis=1), jnp.concatenate([z, w[1]], axis=1)], axis=0)


def _pad_inproj(w):
    D = w.shape[0]
    o = RWKV_COLS + MLA_Q_RANK + MLA_KV_RANK
    return jnp.concatenate(
        [w[:, :o], jnp.zeros((D, 64), w.dtype), w[:, o:o + MLA_ROPE], jnp.zeros((D, 32), w.dtype), w[:, o + MLA_ROPE:]],
        axis=1)


def _pad_wq(w):
    r = w.shape[0]
    w = w.reshape(r, MLA_HEADS, MLA_NOPE + MLA_ROPE)
    w = jnp.concatenate([w, jnp.zeros((r, MLA_HEADS, 32), w.dtype)], axis=2)
    return w.reshape(r, MLA_HEADS * MLA_HEAD_PAD)


def _split_wkv(w):
    r = w.shape[0]
    w = w.reshape(r, MLA_HEADS, MLA_NOPE + MLA_V)
    wk = jnp.concatenate([w[:, :, :MLA_NOPE], jnp.zeros((r, MLA_HEADS, 64), w.dtype)], axis=2)
    return wk.reshape(r, MLA_HEADS * MLA_HEAD_PAD), w[:, :, MLA_NOPE:].reshape(r, MLA_WIDTH)


def kernel(x, c, positions, w_ada, b_ada, w_ff1_in, w_ff1_out, w_ff2_in, w_ff2_out, w_in, w_out, rwkv_mu, rwkv_w0,
           rwkv_w_up, rwkv_a0, rwkv_a_up, rwkv_g_up, rwkv_k_k, rwkv_k_a, rwkv_r_k, rwkv_ln_g, rwkv_ln_b,
           mla_q_norm_g, mla_w_uq, mla_kv_norm_g, mla_w_ukv, ret_log_rate, ret_gn_g, final_norm_g):
    B, S, D = x.shape
    L = w_ada.shape[0]
    mod = _mod_call(c, w_ada, b_ada)
    tabs, tab_t = _rope_call(positions)

    head_id = jnp.arange(256) // HEAD_DIM
    same_head = (head_id[:, None] == head_id[None, :]).astype(F32)
    ones_bd = same_head
    avg_bd = same_head / HEAD_DIM
    final_g = final_norm_g.reshape(1, D)
    w_ff1_in_bf, w_ff1_out_bf = w_ff1_in.astype(BF16), w_ff1_out.astype(BF16)
    w_ff2_in_bf, w_ff2_out_bf = w_ff2_in.astype(BF16), w_ff2_out.astype(BF16)
    w_out_bf = w_out.astype(BF16)
    head_sel = (jnp.arange(MLA_HEADS * MLA_HEAD_PAD)[:, None] // MLA_HEAD_PAD == jnp.arange(128)[None, :]).astype(F32)

    for l in range(L):
        m = [mod[l, :, i * D:(i + 1) * D].reshape(B, 1, D) for i in range(N_MOD)]
        sh1, sc1, g1, sh2, sc2, g2, sh3, sc3, g3 = m

        x = _ffn_call(x, sh1, sc1, g1, w_ff1_in_bf, w_ff1_out_bf, l, final_g, False)

        za, zb, zc = _inproj_call(x, sh2, sc2, _pad_inproj(w_in[l].astype(BF16)), tabs)

        pc, pd, pw, pg = _rwkv_prep_call(
            za, rwkv_mu[l].reshape(1, -1), rwkv_w0[l].reshape(1, 512), _block_diag2(rwkv_w_up[l]),
            rwkv_a0[l].reshape(1, 512), _block_diag2(rwkv_a_up[l]), rwkv_g_up[l],
            rwkv_k_k[l].reshape(1, 256), rwkv_k_a[l].reshape(1, 256), rwkv_r_k[l].reshape(1, 256), ones_bd)
        yf, yb = _rwkv_scan_call(pc, pd, pw)

        wk, wv = _split_wkv(mla_w_ukv[l])
        qt, k, vt, kn = _mla_prep_call(zb, tabs, tab_t, mla_q_norm_g[l].reshape(1, -1),
                                       _pad_wq(mla_w_uq[l]).T.astype(BF16), mla_kv_norm_g[l].reshape(1, -1),
                                       wk.astype(BF16), wv.T.astype(BF16), head_sel)
        o_mla = _mla_attn_call(qt, k, vt, kn)

        lr = ret_log_rate[l]
        lr_vec = jnp.repeat(lr, HEAD_DIM, axis=1)
        lr_heads = jnp.broadcast_to(lr.reshape(8, 1), (8, 128))
        rf, rb = _ret_call(zc, lr_vec, lr_heads)

        x = _outproj_call(x, g2, yf, yb, pg, rwkv_ln_g[l].reshape(1, 256), rwkv_ln_b[l].reshape(1, 256), o_mla,
                          rf, rb, zc, ret_gn_g[l].reshape(1, 256), avg_bd, w_out_bf, l)

        x = _ffn_call(x, sh3, sc3, g3, w_ff2_in_bf, w_ff2_out_bf, l, final_g, l == L - 1)
    return x
```
